```python
import math, functools
import jax, jax.numpy as jnp
from jax import lax
import numpy as np

D_MODEL = 1024
BATCH = 16
SEQ = 4096
DEPTH = 1
DEC_BATCH = 128
DEC_SEQ = 1
PAST_LEN = 8192
PAGE_SIZE = 128

N_HEADS = 8
HEAD_DIM = D_MODEL // (2 * N_HEADS)
V_DIM = 2 * HEAD_DIM
QK_WIDTH = 2 * N_HEADS * HEAD_DIM
V_WIDTH = N_HEADS * V_DIM
POOL_WINDOWS = (2, 4, 8, 16)
POOL_GROUPS = 4
POOL_WIDTH = D_MODEL // 2
POOL_GROUP_DIM = POOL_WIDTH // POOL_GROUPS
POOL_OUT_DIM = D_MODEL // POOL_GROUPS
POOL_STATE_LEN = 15
IN_WIDTH = POOL_WIDTH + 2 * QK_WIDTH + V_WIDTH + 2 * D_MODEL
N_EXPERTS = 32
TOP_K = 4
D_FF = D_MODEL
SWIGLU_LIMIT = 7.0
SWIGLU_ALPHA = 1.702
MOE_BLOCK = 128
ROPE_THETA = 10000.0
Q_BLOCK = 128
RMS_EPS = 1e-6

kernel_name = 'hybrid_pool_diffattn_moe_step'

F32 = jnp.float32


def _rmsnorm(x, g):
    xf = x.astype(F32)
    y = xf * lax.rsqrt(jnp.mean(xf * xf, axis=-1, keepdims=True) + RMS_EPS)
    return (y * g.astype(F32)).astype(x.dtype)


def _rope(x, pos):
    inv = jnp.power(ROPE_THETA, -jnp.arange(0, HEAD_DIM, 2, dtype=F32) / HEAD_DIM)
    ang = pos.astype(F32)[:, None] * inv[None, :]
    cos = jnp.concatenate([jnp.cos(ang), jnp.cos(ang)], -1)[:, None, None, :]
    sin = jnp.concatenate([jnp.sin(ang), jnp.sin(ang)], -1)[:, None, None, :]
    xf = x.astype(F32)
    half = HEAD_DIM // 2
    rot = jnp.concatenate([-xf[..., half:], xf[..., :half]], -1)
    return (xf * cos + rot * sin).astype(x.dtype)


def _project(xn, w_in):
    z = jnp.einsum('bld,de->ble', xn, w_in)
    b, L = z.shape[:2]
    a1 = POOL_WIDTH
    a2 = a1 + QK_WIDTH
    a3 = a2 + QK_WIDTH
    a4 = a3 + V_WIDTH
    a5 = a4 + D_MODEL
    u, q, k, v, ga, gb = jnp.split(z, [a1, a2, a3, a4, a5], axis=-1)
    q = q.reshape(b, L, N_HEADS, 2, HEAD_DIM)
    k = k.reshape(b, L, N_HEADS, 2, HEAD_DIM)
    v = v.reshape(b, L, N_HEADS, V_DIM)
    return u, q, k, v, jax.nn.sigmoid(ga), jax.nn.sigmoid(gb)


def _pool_mixer(u, buf, pos, pool_w, pool_scale):
    b, L = u.shape[:2]
    ext = jnp.concatenate([buf.astype(u.dtype), u], axis=1).astype(F32)
    cs = jnp.concatenate([jnp.zeros_like(ext[:, :1]), jnp.cumsum(ext, axis=1)], axis=1)
    end = cs[:, POOL_STATE_LEN + 1:]
    cur = ext[:, POOL_STATE_LEN:]
    outs = []
    for g, w in enumerate(POOL_WINDOWS):
        sl = slice(g * POOL_GROUP_DIM, (g + 1) * POOL_GROUP_DIM)
        lo = POOL_STATE_LEN + 1 - w
        wsum = end[..., sl] - cs[:, lo:lo + L, sl]
        cnt = jnp.minimum(pos + 1, w).astype(F32)[None, :, None]
        outs.append(wsum / cnt - cur[..., sl])
    pooled = jnp.stack(outs, axis=2)
    y = jnp.einsum('blgc,gco->blgo', pooled, pool_w.astype(F32)).reshape(b, L, D_MODEL)
    y = (y * pool_scale.astype(F32)).astype(u.dtype)
    new_buf = ext[:, -POOL_STATE_LEN:].astype(u.dtype)
    return y, new_buf


def _diff_attn_prompt(q, k, v, lam):
    b, L = q.shape[:2]
    nqb = L // Q_BLOCK
    qb = q.reshape(b, nqb, Q_BLOCK, N_HEADS, 2, HEAD_DIM).transpose(1, 0, 2, 3, 4, 5)
    kpos = jnp.arange(L)
    scale = HEAD_DIM ** -0.5

    def block(args):
        qblk, i = args
        s = jnp.einsum('bqhid,bkhid->bhiqk', qblk, k, preferred_element_type=F32) * scale
        qpos = i * Q_BLOCK + jnp.arange(Q_BLOCK)
        s = jnp.where(kpos[None, :] <= qpos[:, None], s, -jnp.inf)
        p = jax.nn.softmax(s, axis=-1)
        w = p[:, :, 0] - lam * p[:, :, 1]
        return jnp.einsum('bhqk,bkhe->bqhe', w, v, preferred_element_type=F32)

    o = lax.map(block, (qb, jnp.arange(nqb)))
    return o.transpose(1, 0, 2, 3, 4).reshape(b, L, N_HEADS, V_DIM)


def _online_update(carry, s, vals):
    m, l, acc = carry
    m_new = jnp.maximum(m, jnp.max(s, axis=-1))
    corr = jnp.exp(m - m_new)
    p = jnp.exp(s - m_new[..., None])
    l_new = l * corr + jnp.sum(p, axis=-1)
    acc_new = acc * corr[..., None] + jnp.einsum('bhiqk,bkhe->bhiqe', p, vals, preferred_element_type=F32)
    return (m_new, l_new, acc_new)


def _diff_attn_sample(q, k_new, v_new, cache_k, cache_v, page_table, layer, lam):
    db, s_len = q.shape[:2]
    scale = HEAD_DIM ** -0.5

    def page_step(carry, phys):
        kp = cache_k[layer, phys]
        vp = cache_v[layer, phys]
        s = jnp.einsum('bqhid,bkhid->bhiqk', q, kp, preferred_element_type=F32) * scale
        return _online_update(carry, s, vp), None

    init = (jnp.full((db, N_HEADS, 2, s_len), -jnp.inf, F32),
            jnp.zeros((db, N_HEADS, 2, s_len), F32),
            jnp.zeros((db, N_HEADS, 2, s_len, V_DIM), F32))
    carry, _ = lax.scan(page_step, init, page_table.T)
    s_self = jnp.einsum('bqhid,bkhid->bhiqk', q, k_new, preferred_element_type=F32) * scale
    causal = jnp.arange(s_len)[None, :] <= jnp.arange(s_len)[:, None]
    s_self = jnp.where(causal, s_self, -jnp.inf)
    m, l, acc = _online_update(carry, s_self, v_new)
    o = acc / l[..., None]
    o = o[:, :, 0] - lam * o[:, :, 1]
    return o.transpose(0, 2, 1, 3)


def _moe(h, w_router, b_router, w_gate, b_gate, w_up, b_up, w_down, b_down):
    shp = h.shape
    xt = h.reshape(-1, D_MODEL)
    n = xt.shape[0]
    logits = (jnp.einsum('nd,de->ne', xt, w_router, preferred_element_type=F32) + b_router.astype(F32))
    top_val, top_idx = lax.top_k(logits, TOP_K)
    gate_w = jax.nn.softmax(top_val, axis=-1)
    nk = n * TOP_K
    flat_e = top_idx.reshape(-1)
    flat_tok = jnp.arange(nk, dtype=jnp.int32) // TOP_K
    flat_w = gate_w.reshape(-1)
    order = jnp.argsort(flat_e)
    sorted_e = flat_e[order]
    counts = jnp.bincount(flat_e, length=N_EXPERTS)
    padded = (counts + MOE_BLOCK - 1) // MOE_BLOCK * MOE_BLOCK
    start_sorted = jnp.cumsum(counts) - counts
    pad_end = jnp.cumsum(padded)
    start_pad = pad_end - padded
    dest = start_pad[sorted_e] + jnp.arange(nk, dtype=jnp.int32) - start_sorted[sorted_e]
    n_blocks = -(-nk // MOE_BLOCK) + N_EXPERTS
    rows = n_blocks * MOE_BLOCK
    row_tok = jnp.zeros((rows,), jnp.int32).at[dest].set(flat_tok[order])
    row_w = jnp.zeros((rows,), F32).at[dest].set(flat_w[order])
    blk_e = jnp.minimum(jnp.searchsorted(pad_end, jnp.arange(n_blocks) * MOE_BLOCK, side='right'), N_EXPERTS - 1)

    def expert_block(args):
        tok, w, e = args
        xb = xt[tok]
        g = jnp.einsum('nd,df->nf', xb, w_gate[e], preferred_element_type=F32) + b_gate[e].astype(F32)
        u = jnp.einsum('nd,df->nf', xb, w_up[e], preferred_element_type=F32) + b_up[e].astype(F32)
        g = jnp.minimum(g, SWIGLU_LIMIT)
        u = jnp.clip(u, -SWIGLU_LIMIT, SWIGLU_LIMIT)
        act = g * jax.nn.sigmoid(SWIGLU_ALPHA * g) * (u + 1.0)
        yb = jnp.einsum('nf,fd->nd', act, w_down[e].astype(F32)) + b_down[e].astype(F32)
        return yb * w[:, None]

    yb = lax.map(expert_block, (row_tok.reshape(n_blocks, MOE_BLOCK), row_w.reshape(n_blocks, MOE_BLOCK), blk_e))
    out = jnp.zeros((n, D_MODEL), F32).at[row_tok].add(yb.reshape(rows, D_MODEL))
    return out.reshape(shp).astype(h.dtype)


def _layer(h, pos0, pool_buf, attend, lam_init, norm_mix, w_in, pool_w, pool_scale, subln_gain, w_out,
           norm_ffn, w_router, b_router, w_gate, b_gate, w_up, b_up, w_down, b_down):
    b, L = h.shape[:2]
    xn = _rmsnorm(h, norm_mix)
    u, q, k, v, ga, gb = _project(xn, w_in)
    pos = pos0 + jnp.arange(L)
    q = _rope(q, pos)
    k = _rope(k, pos)
    ya, new_buf = _pool_mixer(u, pool_buf, pos, pool_w, pool_scale)
    o = attend(q, k, v)
    yb = (_rmsnorm(o, subln_gain) * (1.0 - lam_init)).reshape(b, L, D_MODEL).astype(h.dtype)
    h = h + jnp.einsum('bld,de->ble', ga * ya + gb * yb, w_out)
    h = h + _moe(_rmsnorm(h, norm_ffn), w_router, b_router, w_gate, b_gate, w_up, b_up, w_down, b_down)
    return h, k, v, new_buf


def setup_inputs(seed: int = 0) -> dict:
    key = jax.random.key(seed)
    ks = jax.random.split(key, 32)
    n_pages = PAST_LEN // PAGE_SIZE
    n_pool = (DEC_BATCH * n_pages * 5) // 4

    def nrm(k, shape, s):
        return s * jax.random.normal(k, shape, F32)

    page_table = jax.random.permutation(ks[5], n_pool)[: DEC_BATCH * n_pages].reshape(DEC_BATCH, n_pages).astype(jnp.int32)
    return {
        'x_prompt': nrm(ks[0], (BATCH, SEQ, D_MODEL), 1.0),
        'x_sample': nrm(ks[1], (DEC_BATCH, DEC_SEQ, D_MODEL), 1.0),
        'cache_k': nrm(ks[2], (DEPTH, n_pool, PAGE_SIZE, N_HEADS, 2, HEAD_DIM), 1.0),
        'cache_v': nrm(ks[3], (DEPTH, n_pool, PAGE_SIZE, N_HEADS, V_DIM), 1.0),
        'state_pool': nrm(ks[4], (DEPTH, DEC_BATCH, POOL_STATE_LEN, POOL_WIDTH), 1.0),
        'page_table': page_table,
        'norm_mix': 1.0 + nrm(ks[6], (DEPTH, D_MODEL), 0.05),
        'w_in': nrm(ks[7], (DEPTH, D_MODEL, IN_WIDTH), D_MODEL ** -0.5),
        'pool_w': nrm(ks[8], (DEPTH, POOL_GROUPS, POOL_GROUP_DIM, POOL_OUT_DIM), POOL_GROUP_DIM ** -0.5),
        'pool_scale': 1.0 + nrm(ks[9], (DEPTH, D_MODEL), 0.1),
        'lambda_q1': nrm(ks[10], (DEPTH, HEAD_DIM), 0.1),
        'lambda_k1': nrm(ks[11], (DEPTH, HEAD_DIM), 0.1),
        'lambda_q2': nrm(ks[12], (DEPTH, HEAD_DIM), 0.1),
        'lambda_k2': nrm(ks[13], (DEPTH, HEAD_DIM), 0.1),
        'subln_gain': 1.0 + nrm(ks[14], (DEPTH, V_DIM), 0.05),
        'w_out': nrm(ks[15], (DEPTH, D_MODEL, D_MODEL), D_MODEL ** -0.5),
        'norm_ffn': 1.0 + nrm(ks[16], (DEPTH, D_MODEL), 0.05),
        'w_router': nrm(ks[17], (DEPTH, D_MODEL, N_EXPERTS), D_MODEL ** -0.5),
        'b_router': nrm(ks[18], (DEPTH, N_EXPERTS), 0.01),
        'w_gate': nrm(ks[19], (DEPTH, N_EXPERTS, D_MODEL, D_FF), D_MODEL ** -0.5),
        'b_gate': nrm(ks[20], (DEPTH, N_EXPERTS, D_FF), 0.01),
        'w_up': nrm(ks[21], (DEPTH, N_EXPERTS, D_MODEL, D_FF), D_MODEL ** -0.5),
        'b_up': nrm(ks[22], (DEPTH, N_EXPERTS, D_FF), 0.01),
        'w_down': nrm(ks[23], (DEPTH, N_EXPERTS, D_FF, D_MODEL), D_FF ** -0.5),
        'b_down': nrm(ks[24], (DEPTH, N_EXPERTS, D_MODEL), 0.01),
        'norm_final': 1.0 + nrm(ks[25], (D_MODEL,), 0.05),
    }


def reference(x_prompt, x_sample, cache_k, cache_v, state_pool, page_table, norm_mix, w_in, pool_w, pool_scale,
              lambda_q1, lambda_k1, lambda_q2, lambda_k2, subln_gain, w_out, norm_ffn, w_router, b_router,
              w_gate, b_gate, w_up, b_up, w_down, b_down, norm_final):
    hp, hs = x_prompt, x_sample
    bp, lp = x_prompt.shape[:2]
    kp_list, vp_list, pp_list, ks_list, vs_list, ps_list = [], [], [], [], [], []
    for layer in range(DEPTH):
        lam_init = 0.8 - 0.6 * math.exp(-0.3 * layer)
        lam = (jnp.exp(jnp.sum(lambda_q1[layer].astype(F32) * lambda_k1[layer].astype(F32)))
               - jnp.exp(jnp.sum(lambda_q2[layer].astype(F32) * lambda_k2[layer].astype(F32))) + lam_init)
        common = (lam_init, norm_mix[layer], w_in[layer], pool_w[layer], pool_scale[layer], subln_gain[layer],
                  w_out[layer], norm_ffn[layer], w_router[layer], b_router[layer], w_gate[layer], b_gate[layer],
                  w_up[layer], b_up[layer], w_down[layer], b_down[layer])
        attend_p = functools.partial(_diff_attn_prompt, lam=lam)
        zero_buf = jnp.zeros((bp, POOL_STATE_LEN, POOL_WIDTH), x_prompt.dtype)
        hp, kp, vp, bufp = _layer(hp, 0, zero_buf, attend_p, *common)
        attend_s = functools.partial(_diff_attn_sample, cache_k=cache_k, cache_v=cache_v,
                                     page_table=page_table, layer=layer, lam=lam)
        hs, ksm, vsm, bufs = _layer(hs, PAST_LEN, state_pool[layer], attend_s, *common)
        kp_list.append(kp.reshape(bp, lp // PAGE_SIZE, PAGE_SIZE, N_HEADS, 2, HEAD_DIM))
        vp_list.append(vp.reshape(bp, lp // PAGE_SIZE, PAGE_SIZE, N_HEADS, V_DIM))
        pp_list.append(bufp)
        ks_list.append(ksm)
        vs_list.append(vsm)
        ps_list.append(bufs)
    y_prompt = _rmsnorm(hp, norm_final)
    y_sample = _rmsnorm(hs, norm_final)
    k_prompt = jnp.stack(kp_list, 0)
    v_prompt = jnp.stack(vp_list, 0)
    pool_prompt = jnp.stack(pp_list, 0)
    k_sample = jnp.stack(ks_list, 0)
    v_sample = jnp.stack(vs_list, 0)
    pool_sample = jnp.stack(ps_list, 0)
    return (y_prompt, y_sample, k_prompt, v_prompt, pool_prompt, k_sample, v_sample, pool_sample)
```

```python
import functools
import math

import jax
import jax.numpy as jnp
from jax import lax
from jax.experimental import pallas as pl
from jax.experimental.pallas import tpu as pltpu

F32 = jnp.float32
BF16 = jnp.bfloat16

N_HEADS = 8
HEAD_DIM = 64
V_DIM = 128
D_MODEL = 1024
POOL_WIDTH = 512
POOL_WINDOWS = (2, 4, 8, 16)
POOL_GROUP_DIM = 128
POOL_OUT_DIM = 256
POOL_STATE_LEN = 15
POOL_HALO = 16
QK_WIDTH = 1024
V_WIDTH = 1024
N_EXPERTS = 32
TOP_K = 4
SWIGLU_LIMIT = 7.0
SWIGLU_ALPHA = 1.702
ROPE_THETA = 10000.0
RMS_EPS = 1e-6
PAST_LEN = 8192
PAGE_SIZE = 128
LAM_INIT = 0.8 - 0.6 * math.exp(-0.3 * 0)

LANES = 128
VMEM_LIMIT = 56 * 2**20

ROW_TILE = 256
Q_TILE = 256
PAGES_PER_STEP = 4
MOE_BLOCK = 512
GATHER_ROWS = 512
COMBINE_TILE = 128


def _cparams(semantics):
    return pltpu.CompilerParams(dimension_semantics=semantics, vmem_limit_bytes=VMEM_LIMIT)


def _rms(x, eps=RMS_EPS):
    return x * lax.rsqrt(jnp.mean(x * x, axis=-1, keepdims=True) + eps)


def _proj_kernel(x_ref, g_ref, w_ref, cos_ref, sin_ref,
                 u_ref, q_ref, kt_ref, kb_ref, v_ref, vb_ref, ga_ref, gb_ref):
    tm = x_ref.shape[0]
    xn = (_rms(x_ref[...]) * g_ref[...]).astype(BF16)

    def sec(a, b):
        return jnp.dot(xn, w_ref[:, a:b], preferred_element_type=F32)

    a1 = POOL_WIDTH
    a2 = a1 + QK_WIDTH
    a3 = a2 + QK_WIDTH
    a4 = a3 + V_WIDTH
    a5 = a4 + D_MODEL
    a6 = a5 + D_MODEL

    u_ref[...] = sec(0, a1)

    cos = cos_ref[...]
    sin = sin_ref[...]
    lane = lax.broadcasted_iota(jnp.int32, (tm, LANES), 1)
    first_half = (lane % HEAD_DIM) < (HEAD_DIM // 2)

    def rope_chunk(zc):
        rot = jnp.where(first_half, pltpu.roll(zc, LANES - HEAD_DIM // 2, 1), pltpu.roll(zc, HEAD_DIM // 2, 1))
        return zc * cos + rot * sin

    zq = sec(a1, a2)
    for c in range(QK_WIDTH // LANES):
        sl = slice(c * LANES, (c + 1) * LANES)
        q_ref[:, sl] = (rope_chunk(zq[:, sl]) * (HEAD_DIM ** -0.5)).astype(BF16)
    zk = sec(a2, a3)
    pg = kt_ref.shape[2]
    for c in range(QK_WIDTH // LANES):
        sl = slice(c * LANES, (c + 1) * LANES)
        r = rope_chunk(zk[:, sl])
        kb_ref[:, sl] = r.astype(BF16)
        for p in range(tm // pg):
            kt_ref[p, sl, :] = r[p * pg:(p + 1) * pg, :].T
    zv = sec(a3, a4)
    vb_ref[...] = zv.astype(BF16)
    for h in range(N_HEADS):
        v_ref[pl.ds(h, tm, stride=N_HEADS), :] = zv[:, h * V_DIM:(h + 1) * V_DIM]
    ga_ref[...] = jax.nn.sigmoid(sec(a4, a5))
    gb_ref[...] = jax.nn.sigmoid(sec(a5, a6))


def _rope_tables(pos):
    inv = jnp.power(ROPE_THETA, -jnp.arange(0, HEAD_DIM, 2, dtype=F32) / HEAD_DIM)
    ang = pos.astype(F32)[:, None] * inv[None, :]
    cos = jnp.concatenate([jnp.cos(ang)] * 4, -1)
    sin = jnp.concatenate([-jnp.sin(ang), jnp.sin(ang)] * 2, -1)
    return cos, sin


def _proj(x2d, norm_g, w_in_bf, cos, sin, tm):
    n = x2d.shape[0]
    n_pos_tiles = cos.shape[0] // tm
    row = lambda i: (i, 0)
    const = lambda i: (0, 0)
    in_width = w_in_bf.shape[1]
    pg = min(PAGE_SIZE, tm)
    outs = [
        jax.ShapeDtypeStruct((n, POOL_WIDTH), F32),
        jax.ShapeDtypeStruct((n, QK_WIDTH), BF16),
        jax.ShapeDtypeStruct((n // pg, QK_WIDTH, pg), F32),
        jax.ShapeDtypeStruct((n, QK_WIDTH), BF16),
        jax.ShapeDtypeStruct((n * N_HEADS, V_DIM), F32),
        jax.ShapeDtypeStruct((n, V_WIDTH), BF16),
        jax.ShapeDtypeStruct((n, D_MODEL), F32),
        jax.ShapeDtypeStruct((n, D_MODEL), F32),
    ]
    out_specs = [pl.BlockSpec((tm, o.shape[1]), row) for o in outs]
    out_specs[2] = pl.BlockSpec((tm // pg, QK_WIDTH, pg), lambda i: (i, 0, 0))
    out_specs[4] = pl.BlockSpec((tm * N_HEADS, V_DIM), row)
    return pl.pallas_call(
        _proj_kernel,
        out_shape=outs,
        grid=(n // tm,),
        in_specs=[
            pl.BlockSpec((tm, D_MODEL), row),
            pl.BlockSpec((1, D_MODEL), const),
            pl.BlockSpec((D_MODEL, in_width), const),
            pl.BlockSpec((tm, LANES), lambda i: (i % n_pos_tiles, 0)),
            pl.BlockSpec((tm, LANES), lambda i: (i % n_pos_tiles, 0)),
        ],
        out_specs=out_specs,
        compiler_params=_cparams(("parallel",)),
        name="proj",
    )(x2d, norm_g.reshape(1, D_MODEL), w_in_bf, cos, sin)


def _lam_value(lamv_ref):
    lv = lamv_ref[...]
    a = jnp.sum(lv[0:1] * lv[1:2], axis=-1, keepdims=True)
    b = jnp.sum(lv[2:3] * lv[3:4], axis=-1, keepdims=True)
    return jnp.exp(a) - jnp.exp(b) + LAM_INIT


def _subln(o, gain):
    return (_rms(o) * gain) * (1.0 - LAM_INIT)


def _attn_kernel(lamv_ref, gain_ref, q_ref, k_ref, v_ref, o_ref):
    tq = q_ref.shape[0]
    i = pl.program_id(2)
    q = q_ref[...]
    lane = lax.broadcasted_iota(jnp.int32, (tq, LANES), 1)
    zero = jnp.zeros_like(q)
    qq = jnp.concatenate([jnp.where(lane < HEAD_DIM, q, zero), jnp.where(lane >= HEAD_DIM, q, zero)], axis=0)

    def tile(j, carry, masked):
        m, l, acc = carry
        start = pl.multiple_of(j * tq, tq)
        ks = k_ref[pl.ds(start, tq), :]
        vs = v_ref[pl.ds(start, tq), :]
        s = lax.dot_general(qq, ks, (((1,), (1,)), ((), ())), preferred_element_type=F32)
        if masked:
            r = lax.broadcasted_iota(jnp.int32, (2 * tq, tq), 0)
            c = lax.broadcasted_iota(jnp.int32, (2 * tq, tq), 1)
            r = jnp.where(r >= tq, r - tq, r)
            s = jnp.where(c <= r, s, -jnp.inf)
        m_new = jnp.maximum(m, jnp.max(s, axis=1, keepdims=True))
        alpha = jnp.exp(m - m_new)
        p = jnp.exp(s - m_new)
        l = alpha * l + jnp.sum(p, axis=1, keepdims=True)
        acc = alpha * acc + jnp.dot(p.astype(BF16), vs, preferred_element_type=F32)
        return m_new, l, acc

    init = (jnp.full((2 * tq, 1), -jnp.inf, F32), jnp.zeros((2 * tq, 1), F32), jnp.zeros((2 * tq, V_DIM), F32))
    carry = lax.fori_loop(0, i, lambda j, c: tile(j, c, False), init)
    m, l, acc = tile(i, carry, True)
    o = acc / l
    o = o[:tq] - _lam_value(lamv_ref) * o[tq:]
    o_ref[...] = _subln(o, gain_ref[...])


def _attn_prompt(lamv, gain, qb, kb, vb, tq):
    b, L, _ = qb.shape
    kv_spec = pl.BlockSpec((None, L, LANES), lambda bi, h, i: (bi, 0, h))
    return pl.pallas_call(
        _attn_kernel,
        out_shape=jax.ShapeDtypeStruct((b, L, V_WIDTH), F32),
        grid=(b, N_HEADS, L // tq),
        in_specs=[
            pl.BlockSpec((4, HEAD_DIM), lambda bi, h, i: (0, 0)),
            pl.BlockSpec((1, V_DIM), lambda bi, h, i: (0, 0)),
            pl.BlockSpec((None, tq, LANES), lambda bi, h, i: (bi, i, h)),
            kv_spec, kv_spec,
        ],
        out_specs=pl.BlockSpec((None, tq, LANES), lambda bi, h, i: (bi, i, h)),
        compiler_params=_cparams(("parallel", "parallel", "arbitrary")),
        name="attn_prompt",
    )(lamv, gain, qb, kb, vb)


def _decode_kernel(pt_ref, lamv_ref, gain_ref, q_ref, kn_ref, vn_ref, *rest):
    g = (len(rest) - 5) // 2
    k_refs = rest[:g]
    v_refs = rest[g:2 * g]
    o_ref = rest[2 * g]
    qbd_ref, m_ref, l_ref, acc_ref = rest[2 * g + 1:]
    n_maps = 2 * N_HEADS
    j = pl.program_id(1)

    @pl.when(j == 0)
    def _():
        row = lax.broadcasted_iota(jnp.int32, (n_maps, QK_WIDTH), 0)
        col = lax.broadcasted_iota(jnp.int32, (n_maps, QK_WIDTH), 1)
        qrow = jnp.broadcast_to(q_ref[...].astype(F32), (n_maps, QK_WIDTH))
        qbd_ref[...] = jnp.where(col // HEAD_DIM == row, qrow, 0.0).astype(BF16)
        m_ref[...] = jnp.full(m_ref.shape, -jnp.inf, F32)
        l_ref[...] = jnp.zeros(l_ref.shape, F32)
        acc_ref[...] = jnp.zeros(acc_ref.shape, F32)

    qbd = qbd_ref[...]
    m = m_ref[...]
    l = l_ref[...]
    acc = acc_ref[...]
    for t in range(g):
        kt = k_refs[t][...].astype(BF16)
        s = jnp.dot(qbd, kt, preferred_element_type=F32)
        m_new = jnp.maximum(m, jnp.max(s, axis=1, keepdims=True))
        alpha = jnp.exp(m - m_new)
        p = jnp.exp(s - m_new).astype(BF16)
        l = alpha * l + jnp.sum(p.astype(F32), axis=1, keepdims=True)
        pv = [jnp.dot(p, v_refs[t][pl.ds(h, PAGE_SIZE, stride=N_HEADS), :].astype(BF16), preferred_element_type=F32)
              for h in range(N_HEADS)]
        acc = alpha * acc + jnp.concatenate(pv, axis=1)
        m = m_new
    m_ref[...] = m
    l_ref[...] = l
    acc_ref[...] = acc

    @pl.when(j == pl.num_programs(1) - 1)
    def _():
        kn = kn_ref[...].astype(F32)
        vn = vn_ref[...].astype(F32)
        s = jnp.sum(qbd.astype(F32) * kn, axis=1, keepdims=True)
        m_new = jnp.maximum(m, s)
        alpha = jnp.exp(m - m_new)
        p = jnp.exp(s - m_new)
        lf = alpha * l + p
        accf = alpha * acc + p.astype(BF16).astype(F32) * vn
        o = accf / lf
        lam = _lam_value(lamv_ref)
        gain = gain_ref[...]
        for h in range(N_HEADS):
            sl = slice(h * V_DIM, (h + 1) * V_DIM)
            oh = o[2 * h:2 * h + 1, sl] - lam * o[2 * h + 1:2 * h + 2, sl]
            o_ref[:, sl] = _subln(oh, gain)


def _attn_sample(page_table, lamv, gain, qb, k_new, v_new, cache_k, cache_v):
    db, n_pages = page_table.shape
    g = min(PAGES_PER_STEP, n_pages)
    n_maps = 2 * N_HEADS
    row_spec = pl.BlockSpec((None, 1, QK_WIDTH), lambda b, j, pt: (b, 0, 0))

    def page_spec(t):
        return pl.BlockSpec((None, QK_WIDTH, PAGE_SIZE), lambda b, j, pt: (pt[b, j * g + t], 0, 0))

    grid_spec = pltpu.PrefetchScalarGridSpec(
        num_scalar_prefetch=1,
        grid=(db, n_pages // g),
        in_specs=[
            pl.BlockSpec((4, HEAD_DIM), lambda b, j, pt: (0, 0)),
            pl.BlockSpec((1, V_DIM), lambda b, j, pt: (0, 0)),
            row_spec, row_spec, row_spec,
        ] + [page_spec(t) for t in range(g)] + [page_spec(t) for t in range(g)],
        out_specs=row_spec,
        scratch_shapes=[
            pltpu.VMEM((n_maps, QK_WIDTH), BF16),
            pltpu.VMEM((n_maps, 1), F32),
            pltpu.VMEM((n_maps, 1), F32),
            pltpu.VMEM((n_maps, V_WIDTH), F32),
        ],
    )
    out = pl.pallas_call(
        _decode_kernel,
        out_shape=jax.ShapeDtypeStruct((db, 1, V_WIDTH), F32),
        grid_spec=grid_spec,
        compiler_params=_cparams(("parallel", "arbitrary")),
        name="attn_sample",
    )(page_table, lamv, gain, qb.reshape(db, 1, QK_WIDTH), k_new.reshape(db, 1, QK_WIDTH),
      v_new.reshape(db, 1, V_WIDTH), *([cache_k] * g), *([cache_v] * g))
    return out.reshape(db, V_WIDTH)


def _merge_kernel(*refs, prompt, tiles_per_seq):
    if prompt:
        (h_ref, u_ref, halo_ref, yb_ref, ga_ref, gb_ref, pw_ref, ps_ref, wo_ref, nf_ref, wr_ref, br_ref,
         h2_ref, xn_ref, ti_ref, tw_ref, ext_ref) = refs
    else:
        (h_ref, u_ref, st_ref, yb_ref, ga_ref, gb_ref, pw_ref, ps_ref, wo_ref, nf_ref, wr_ref, br_ref,
         h2_ref, xn_ref, ti_ref, tw_ref) = refs
    tm = h_ref.shape[0]
    u = u_ref[...]

    if prompt:
        seq_tile = pl.program_id(0) % tiles_per_seq
        halo = jnp.where(seq_tile == 0, 0.0, halo_ref[...])
        ext_ref[0:POOL_HALO, :] = halo
        ext_ref[POOL_HALO:, :] = u
        pos = seq_tile * tm + lax.broadcasted_iota(jnp.int32, (tm, 1), 0)

    ya_parts = []
    for gi, w in enumerate(POOL_WINDOWS):
        sl = slice(gi * POOL_GROUP_DIM, (gi + 1) * POOL_GROUP_DIM)
        cur = u[:, sl]
        wsum = cur
        if prompt:
            for d in range(1, w):
                wsum = wsum + ext_ref[POOL_HALO - d:POOL_HALO - d + tm, sl]
            cnt = jnp.minimum(pos + 1, w).astype(F32)
        else:
            for d in range(1, w):
                wsum = wsum + st_ref[POOL_STATE_LEN - d][:, sl]
            cnt = float(w)
        pooled = wsum / cnt - cur
        ya_parts.append(jnp.dot(pooled.astype(BF16), pw_ref[gi], preferred_element_type=F32))
    ya = jnp.concatenate(ya_parts, axis=1) * ps_ref[...]

    mix = ga_ref[...] * ya + gb_ref[...] * yb_ref[...]
    h2 = h_ref[...] + jnp.dot(mix.astype(BF16), wo_ref[...], preferred_element_type=F32)
    h2_ref[...] = h2
    xn = _rms(h2) * nf_ref[...]
    xn_ref[...] = xn

    wr = wr_ref[...]
    wr_hi = wr.astype(BF16)
    wr_lo = (wr - wr_hi.astype(F32)).astype(BF16)
    xn_hi = xn.astype(BF16)
    xn_lo = (xn - xn_hi.astype(F32)).astype(BF16)
    logits = (jnp.dot(xn_hi, wr_hi, preferred_element_type=F32)
              + jnp.dot(xn_lo, wr_hi, preferred_element_type=F32)
              + jnp.dot(xn_hi, wr_lo, preferred_element_type=F32)) + br_ref[...]

    lane = lax.broadcasted_iota(jnp.int32, (tm, N_EXPERTS), 1).astype(F32)
    slot = lax.broadcasted_iota(jnp.int32, (tm, TOP_K), 1)
    vals = logits
    top_i = jnp.zeros((tm, TOP_K), F32)
    top_v = jnp.zeros((tm, TOP_K), F32)
    for kk in range(TOP_K):
        mx = jnp.max(vals, axis=1, keepdims=True)
        am = jnp.min(jnp.where(vals == mx, lane, float(N_EXPERTS)), axis=1, keepdims=True)
        top_i = jnp.where(slot == kk, am, top_i)
        top_v = jnp.where(slot == kk, mx, top_v)
        vals = jnp.where(lane == am, -jnp.inf, vals)
    e = jnp.exp(top_v - jnp.max(top_v, axis=1, keepdims=True))
    ti_ref[...] = top_i.astype(jnp.int32)
    tw_ref[...] = e / jnp.sum(e, axis=1, keepdims=True)


def _merge(h, u, hist, yb, ga, gb, pool_w_bf, pool_scale, w_out_bf, norm_ffn, w_router, b_router, tm, prompt,
           tiles_per_seq):
    n = h.shape[0]
    row = lambda i: (i, 0)
    const2 = lambda i: (0, 0)
    if prompt:
        per = tm // POOL_HALO
        hist_spec = pl.BlockSpec((POOL_HALO, POOL_WIDTH), lambda i: (jnp.maximum(i * per - 1, 0), 0))
        scratch = [pltpu.VMEM((POOL_HALO + tm, POOL_WIDTH), F32)]
    else:
        hist_spec = pl.BlockSpec((POOL_STATE_LEN, tm, POOL_WIDTH), lambda i: (0, i, 0))
        scratch = []
    outs = [
        jax.ShapeDtypeStruct((n, D_MODEL), F32),
        jax.ShapeDtypeStruct((n, D_MODEL), F32),
        jax.ShapeDtypeStruct((n, TOP_K), jnp.int32),
        jax.ShapeDtypeStruct((n, TOP_K), F32),
    ]
    return pl.pallas_call(
        functools.partial(_merge_kernel, prompt=prompt, tiles_per_seq=tiles_per_seq),
        out_shape=outs,
        grid=(n // tm,),
        in_specs=[
            pl.BlockSpec((tm, D_MODEL), row),
            pl.BlockSpec((tm, POOL_WIDTH), row),
            hist_spec,
            pl.BlockSpec((tm, D_MODEL), row),
            pl.BlockSpec((tm, D_MODEL), row),
            pl.BlockSpec((tm, D_MODEL), row),
            pl.BlockSpec(pool_w_bf.shape, lambda i: (0, 0, 0)),
            pl.BlockSpec((1, D_MODEL), const2),
            pl.BlockSpec((D_MODEL, D_MODEL), const2),
            pl.BlockSpec((1, D_MODEL), const2),
            pl.BlockSpec((D_MODEL, N_EXPERTS), const2),
            pl.BlockSpec((1, N_EXPERTS), const2),
        ],
        out_specs=[pl.BlockSpec((tm, o.shape[1]), row) for o in outs],
        scratch_shapes=scratch,
        compiler_params=_cparams(("parallel",)),
        name="merge_prompt" if prompt else "merge_sample",
    )(h, u, hist, yb, ga, gb, pool_w_bf, pool_scale.reshape(1, D_MODEL), w_out_bf,
      norm_ffn.reshape(1, D_MODEL), w_router, b_router.reshape(1, N_EXPERTS))


def _gather_kernel(tok_ref, x_hbm, o_ref, sem):
    rows = o_ref.shape[0]

    def issue(r, _):
        pltpu.make_async_copy(x_hbm.at[pl.ds(tok_ref[0, r], 1), :], o_ref.at[pl.ds(r, 1), :], sem).start()
        return 0

    lax.fori_loop(0, rows, issue, 0)
    pltpu.make_async_copy(o_ref, o_ref, sem).wait()


def _gather_rows(x, row_tok, rows_per_step):
    n_rows = row_tok.shape[0]
    steps = n_rows // rows_per_step
    width = x.shape[1]
    return pl.pallas_call(
        _gather_kernel,
        out_shape=jax.ShapeDtypeStruct((n_rows, width), x.dtype),
        grid=(steps,),
        in_specs=[
            pl.BlockSpec((None, 1, rows_per_step), lambda i: (i, 0, 0), memory_space=pltpu.SMEM),
            pl.BlockSpec(memory_space=pl.ANY),
        ],
        out_specs=pl.BlockSpec((rows_per_step, width), lambda i: (i, 0)),
        scratch_shapes=[pltpu.SemaphoreType.DMA],
        compiler_params=_cparams(("arbitrary",)),
        name="moe_gather",
    )(row_tok.reshape(steps, 1, rows_per_step), x)


def _ffn_kernel(be_ref, nu_ref, x_ref, wg_ref, bg_ref, wu_ref, bu_ref, wd_ref, bd_ref, y_ref):
    i = pl.program_id(0)

    @pl.when(i < nu_ref[0])
    def _():
        xb = x_ref[...].astype(BF16)
        g = jnp.dot(xb, wg_ref[...], preferred_element_type=F32) + bg_ref[...]
        u = jnp.dot(xb, wu_ref[...], preferred_element_type=F32) + bu_ref[...]
        g = jnp.minimum(g, SWIGLU_LIMIT)
        u = jnp.clip(u, -SWIGLU_LIMIT, SWIGLU_LIMIT)
        act = g * jax.nn.sigmoid(SWIGLU_ALPHA * g) * (u + 1.0)
        y_ref[...] = jnp.dot(act.astype(BF16), wd_ref[...], preferred_element_type=F32) + bd_ref[...]

    @pl.when(i >= nu_ref[0])
    def _():
        y_ref[...] = jnp.zeros(y_ref.shape, F32)


def _ffn(blk_e, n_used, xs, wg, bg, wu, bu, wd, bd, bm):
    rows = xs.shape[0]
    w_spec = pl.BlockSpec((None, D_MODEL, D_MODEL), lambda i, be, nu: (be[i], 0, 0))
    b_spec = pl.BlockSpec((None, 1, D_MODEL), lambda i, be, nu: (be[i], 0, 0))
    x_spec = pl.BlockSpec((bm, D_MODEL), lambda i, be, nu: (i, 0))
    grid_spec = pltpu.PrefetchScalarGridSpec(
        num_scalar_prefetch=2,
        grid=(rows // bm,),
        in_specs=[x_spec, w_spec, b_spec, w_spec, b_spec, w_spec, b_spec],
        out_specs=x_spec,
    )
    e = bg.shape[0]
    return pl.pallas_call(
        _ffn_kernel,
        out_shape=jax.ShapeDtypeStruct((rows, D_MODEL), F32),
        grid_spec=grid_spec,
        compiler_params=_cparams(("arbitrary",)),
        name="moe_ffn",
    )(blk_e, n_used, xs, wg, bg.reshape(e, 1, D_MODEL), wu, bu.reshape(e, 1, D_MODEL), wd, bd.reshape(e, 1, D_MODEL))


def _combine_kernel(dest_ref, h_ref, w_ref, nf_ref, ys_hbm, y_ref, buf_ref, sem):
    tn = h_ref.shape[0]

    def issue(t, _):
        for kk in range(TOP_K):
            pltpu.make_async_copy(ys_hbm.at[pl.ds(dest_ref[0, t * TOP_K + kk], 1), :],
                                  buf_ref.at[kk, pl.ds(t, 1), :], sem).start()
        return 0

    lax.fori_loop(0, tn, issue, 0)
    for kk in range(TOP_K):
        pltpu.make_async_copy(buf_ref.at[kk], buf_ref.at[kk], sem).wait()
    w = w_ref[...]
    out = h_ref[...]
    for kk in range(TOP_K):
        out = out + w[:, kk:kk + 1] * buf_ref[kk]
    y_ref[...] = _rms(out) * nf_ref[...]


def _combine(dest, h2, gate_w, norm_final, ys, tn):
    n = h2.shape[0]
    steps = n // tn
    return pl.pallas_call(
        _combine_kernel,
        out_shape=jax.ShapeDtypeStruct((n, D_MODEL), F32),
        grid=(steps,),
        in_specs=[
            pl.BlockSpec((None, 1, tn * TOP_K), lambda i: (i, 0, 0), memory_space=pltpu.SMEM),
            pl.BlockSpec((tn, D_MODEL), lambda i: (i, 0)),
            pl.BlockSpec((tn, TOP_K), lambda i: (i, 0)),
            pl.BlockSpec((1, D_MODEL), lambda i: (0, 0)),
            pl.BlockSpec(memory_space=pl.ANY),
        ],
        out_specs=pl.BlockSpec((tn, D_MODEL), lambda i: (i, 0)),
        scratch_shapes=[pltpu.VMEM((TOP_K, tn, D_MODEL), F32), pltpu.SemaphoreType.DMA],
        compiler_params=_cparams(("arbitrary",)),
        name="moe_combine",
    )(dest.reshape(steps, 1, tn * TOP_K), h2, gate_w, norm_final.reshape(1, D_MODEL), ys)


def _moe_final(h2, xn, top_i, top_w, wg, bg, wu, bu, wd, bd, norm_final):
    n = h2.shape[0]
    nk = n * TOP_K
    bm = MOE_BLOCK
    sel = (top_i[:, :, None] == jnp.arange(N_EXPERTS, dtype=jnp.int32)[None, None, :]).astype(jnp.int32).sum(1)
    incl = jnp.cumsum(sel, axis=0)
    counts = incl[-1]
    rank = jnp.take_along_axis(incl - sel, top_i, axis=1)
    padded = (counts + bm - 1) // bm * bm
    pad_end = jnp.cumsum(padded)
    start_pad = pad_end - padded
    dest = (start_pad[top_i] + rank).astype(jnp.int32)
    n_blocks = -(-nk // bm) + N_EXPERTS
    rows = n_blocks * bm
    rows = -(-rows // GATHER_ROWS) * GATHER_ROWS
    n_blocks = rows // bm
    tok = jnp.broadcast_to(jnp.arange(n, dtype=jnp.int32)[:, None], (n, TOP_K))
    row_tok = jnp.zeros((rows,), jnp.int32).at[dest.reshape(-1)].set(tok.reshape(-1))
    blk_e = jnp.minimum(jnp.searchsorted(pad_end, jnp.arange(n_blocks, dtype=jnp.int32) * bm, side='right'),
                        N_EXPERTS - 1).astype(jnp.int32)
    n_used = (pad_end[-1] // bm).astype(jnp.int32).reshape(1)

    xs = _gather_rows(xn, row_tok, GATHER_ROWS)
    ys = _ffn(blk_e, n_used, xs, wg, bg, wu, bu, wd, bd, bm)
    tn = min(COMBINE_TILE, n)
    return _combine(dest, h2, top_w, norm_final, ys, tn)


def kernel(x_prompt, x_sample, cache_k, cache_v, state_pool, page_table, norm_mix, w_in, pool_w, pool_scale,
           lambda_q1, lambda_k1, lambda_q2, lambda_k2, subln_gain, w_out, norm_ffn, w_router, b_router,
           w_gate, b_gate, w_up, b_up, w_down, b_down, norm_final):
    bp, lp, d = x_prompt.shape
    db = x_sample.shape[0]
    layer = 0
    n_pool = cache_k.shape[1]

    w_in_bf = w_in[layer].astype(BF16)
    w_out_bf = w_out[layer].astype(BF16)
    pool_w_bf = pool_w[layer].astype(BF16)
    wg = w_gate[layer].astype(BF16)
    wu = w_up[layer].astype(BF16)
    wd = w_down[layer].astype(BF16)
    lamv = jnp.stack([lambda_q1[layer], lambda_k1[layer], lambda_q2[layer], lambda_k2[layer]]).astype(F32)
    gain = subln_gain[layer].reshape(1, V_DIM)
    moe_w = (wg, b_gate[layer], wu, b_up[layer], wd, b_down[layer])

    tm = min(ROW_TILE, lp)
    tq = min(Q_TILE, lp)
    cos_p, sin_p = _rope_tables(jnp.arange(lp))
    xp = x_prompt.reshape(bp * lp, d)
    u, qb, kt, kb, v, vb, ga, gb = _proj(xp, norm_mix[layer], w_in_bf, cos_p, sin_p, tm)
    yb = _attn_prompt(lamv, gain, qb.reshape(bp, lp, QK_WIDTH), kb.reshape(bp, lp, QK_WIDTH),
                      vb.reshape(bp, lp, V_WIDTH), tq).reshape(bp * lp, V_WIDTH)
    h2, xn, top_i, top_w = _merge(xp, u, u, yb, ga, gb, pool_w_bf, pool_scale[layer], w_out_bf, norm_ffn[layer],
                                  w_router[layer], b_router[layer], tm, True, lp // tm)
    y_prompt = _moe_final(h2, xn, top_i, top_w, *moe_w, norm_final).reshape(bp, lp, d)
    pg = kt.shape[2]
    k_prompt = jnp.transpose(kt.reshape(bp, lp // pg, N_HEADS, 2, HEAD_DIM, pg), (0, 1, 5, 2, 3, 4))
    k_prompt = k_prompt.reshape(1, bp, lp // PAGE_SIZE, PAGE_SIZE, N_HEADS, 2, HEAD_DIM)
    v_prompt = v.reshape(1, bp, lp // PAGE_SIZE, PAGE_SIZE, N_HEADS, V_DIM)
    pool_prompt = u.reshape(bp, lp, POOL_WIDTH)[:, lp - POOL_STATE_LEN:, :][None]

    cos_s, sin_s = _rope_tables(jnp.full((db,), PAST_LEN))
    xs = x_sample.reshape(db, d)
    us, qbs, kts, kbs, vs, vbs, gas, gbs = _proj(xs, norm_mix[layer], w_in_bf, cos_s, sin_s, db)
    cache_kt = jnp.transpose(cache_k[layer], (0, 2, 3, 4, 1)).reshape(n_pool, QK_WIDTH, PAGE_SIZE)
    cache_vr = cache_v[layer].reshape(n_pool, PAGE_SIZE * N_HEADS, V_DIM)
    ybs = _attn_sample(page_table, lamv, gain, qbs, kbs, vbs, cache_kt, cache_vr)
    state_t = jnp.transpose(state_pool[layer], (1, 0, 2))
    h2s, xns, tis, tws = _merge(xs, us, state_t, ybs, gas, gbs, pool_w_bf, pool_scale[layer], w_out_bf,
                                norm_ffn[layer], w_router[layer], b_router[layer], db, False, 1)
    y_sample = _moe_final(h2s, xns, tis, tws, *moe_w, norm_final).reshape(db, 1, d)
    k_sample = jnp.transpose(kts.reshape(QK_WIDTH, db), (1, 0)).reshape(1, db, 1, N_HEADS, 2, HEAD_DIM)
    v_sample = vs.reshape(1, db, 1, N_HEADS, V_DIM)
    pool_sample = jnp.concatenate([state_pool[layer][:, 1:, :], us[:, None, :]], axis=1)[None]

    return (y_prompt, y_sample, k_prompt, v_prompt, pool_prompt, k_sample, v_sample, pool_sample)
```

```python
import functools
import math

import numpy as np
import jax
import jax.numpy as jnp
from jax import lax
from jax.experimental import pallas as pl
from jax.experimental.pallas import tpu as pltpu

F32 = jnp.float32
BF16 = jnp.bfloat16

N_HEADS = 8
HEAD_DIM = 64
V_DIM = 128
D_MODEL = 1024
POOL_WIDTH = 512
POOL_WINDOWS = (2, 4, 8, 16)
POOL_GROUP_DIM = 128
POOL_OUT_DIM = 256
POOL_STATE_LEN = 15
POOL_HALO = 16
QK_WIDTH = 1024
V_WIDTH = 1024
N_EXPERTS = 32
TOP_K = 4
SWIGLU_LIMIT = 7.0
SWIGLU_ALPHA = 1.702
ROPE_THETA = 10000.0
RMS_EPS = 1e-6
PAST_LEN = 8192
PAGE_SIZE = 128
LAM_INIT = 0.8 - 0.6 * math.exp(-0.3 * 0)

LANES = 128
SUBLANES = 8
ROW_CHUNKS = D_MODEL // LANES
VMEM_LIMIT = 56 * 2**20

ROW_TILE = 256
Q_TILE = 512
SOFTMAX_ROWS = 32
PAGES_PER_STEP = 4
MOE_BLOCK = 512
MOE_TOKENS = 128
M_FLOOR = -1e30


def _cparams(semantics):
    return pltpu.CompilerParams(dimension_semantics=semantics, vmem_limit_bytes=VMEM_LIMIT)


def _rms(x, eps=RMS_EPS):
    return x * lax.rsqrt(jnp.mean(x * x, axis=-1, keepdims=True) + eps)


def _chunk_rows(ref, c, n):
    return ref[pl.ds(c, n, stride=ROW_CHUNKS), :]


def _proj_kernel(x_ref, g_ref, w_ref, cos_ref, sin_ref,
                 u_ref, q_ref, kt_ref, kb_ref, v_ref, vb_ref, ga_ref, gb_ref, *, kb_transposed):
    tm = x_ref.shape[0]
    xn = (_rms(x_ref[...]) * g_ref[...]).astype(BF16)

    def sec(a, b):
        return jnp.dot(xn, w_ref[:, a:b], preferred_element_type=F32)

    a1 = POOL_WIDTH
    a2 = a1 + QK_WIDTH
    a3 = a2 + QK_WIDTH
    a4 = a3 + V_WIDTH
    a5 = a4 + D_MODEL
    a6 = a5 + D_MODEL

    u_ref[...] = sec(0, a1)

    cos = cos_ref[...]
    sin = sin_ref[...]
    lane = lax.broadcasted_iota(jnp.int32, (tm, LANES), 1)
    first_half = (lane % HEAD_DIM) < (HEAD_DIM // 2)

    def rope_chunk(zc):
        rot = jnp.where(first_half, pltpu.roll(zc, LANES - HEAD_DIM // 2, 1), pltpu.roll(zc, HEAD_DIM // 2, 1))
        return zc * cos + rot * sin

    zq = sec(a1, a2)
    for c in range(QK_WIDTH // LANES):
        sl = slice(c * LANES, (c + 1) * LANES)
        q_ref[:, sl] = (rope_chunk(zq[:, sl]) * (HEAD_DIM ** -0.5)).astype(BF16)
    zk = sec(a2, a3)
    pg = kt_ref.shape[2]
    for c in range(QK_WIDTH // LANES):
        sl = slice(c * LANES, (c + 1) * LANES)
        r = rope_chunk(zk[:, sl])
        if not kb_transposed:
            kb_ref[:, sl] = r.astype(BF16)
        for p in range(tm // pg):
            rt = r[p * pg:(p + 1) * pg, :].T
            kt_ref[p, sl, :] = rt
            if kb_transposed:
                kb_ref[c, :, p * pg:(p + 1) * pg] = rt.astype(BF16)
    zv = sec(a3, a4)
    vb_ref[...] = zv.astype(BF16)
    for h in range(N_HEADS):
        v_ref[pl.ds(h, tm, stride=N_HEADS), :] = zv[:, h * V_DIM:(h + 1) * V_DIM]
    ga_ref[...] = jax.nn.sigmoid(sec(a4, a5))
    gb_ref[...] = jax.nn.sigmoid(sec(a5, a6))


def _rope_tables(pos):
    inv = jnp.power(ROPE_THETA, -jnp.arange(0, HEAD_DIM, 2, dtype=F32) / HEAD_DIM)
    ang = pos.astype(F32)[:, None] * inv[None, :]
    cos = jnp.concatenate([jnp.cos(ang)] * 4, -1)
    sin = jnp.concatenate([-jnp.sin(ang), jnp.sin(ang)] * 2, -1)
    return cos, sin


def _proj(x2d, norm_g, w_in_bf, cos, sin, tm, seq_len):
    n = x2d.shape[0]
    n_pos_tiles = cos.shape[0] // tm
    row = lambda i: (i, 0)
    const = lambda i: (0, 0)
    in_width = w_in_bf.shape[1]
    pg = min(PAGE_SIZE, tm)
    kb_transposed = seq_len is not None
    if kb_transposed:
        tps = seq_len // tm
        kb_shape = jax.ShapeDtypeStruct((n // seq_len, N_HEADS, 2 * HEAD_DIM, seq_len), BF16)
        kb_spec = pl.BlockSpec((None, N_HEADS, 2 * HEAD_DIM, tm), lambda i: (i // tps, 0, 0, i % tps))
    else:
        kb_shape = jax.ShapeDtypeStruct((n, QK_WIDTH), BF16)
        kb_spec = pl.BlockSpec((tm, QK_WIDTH), row)
    outs = [
        jax.ShapeDtypeStruct((n, POOL_WIDTH), F32),
        jax.ShapeDtypeStruct((n, QK_WIDTH), BF16),
        jax.ShapeDtypeStruct((n // pg, QK_WIDTH, pg), F32),
        kb_shape,
        jax.ShapeDtypeStruct((n * N_HEADS, V_DIM), F32),
        jax.ShapeDtypeStruct((n, V_WIDTH), BF16),
        jax.ShapeDtypeStruct((n, D_MODEL), F32),
        jax.ShapeDtypeStruct((n, D_MODEL), F32),
    ]
    out_specs = [pl.BlockSpec((tm, o.shape[-1]), row) for o in outs]
    out_specs[2] = pl.BlockSpec((tm // pg, QK_WIDTH, pg), lambda i: (i, 0, 0))
    out_specs[3] = kb_spec
    out_specs[4] = pl.BlockSpec((tm * N_HEADS, V_DIM), row)
    return pl.pallas_call(
        functools.partial(_proj_kernel, kb_transposed=kb_transposed),
        out_shape=outs,
        grid=(n // tm,),
        in_specs=[
            pl.BlockSpec((tm, D_MODEL), row),
            pl.BlockSpec((1, D_MODEL), const),
            pl.BlockSpec((D_MODEL, in_width), const),
            pl.BlockSpec((tm, LANES), lambda i: (i % n_pos_tiles, 0)),
            pl.BlockSpec((tm, LANES), lambda i: (i % n_pos_tiles, 0)),
        ],
        out_specs=out_specs,
        compiler_params=_cparams(("parallel",)),
        name="proj",
    )(x2d, norm_g.reshape(1, D_MODEL), w_in_bf, cos, sin)


def _lam_value(lamv_ref):
    lv = lamv_ref[...]
    a = jnp.sum(lv[0:1] * lv[1:2], axis=-1, keepdims=True)
    b = jnp.sum(lv[2:3] * lv[3:4], axis=-1, keepdims=True)
    return jnp.exp(a) - jnp.exp(b) + LAM_INIT


def _subln(o, gain):
    return (_rms(o) * gain) * (1.0 - LAM_INIT)


def _attn_kernel(tab_ref, lamv_ref, gain_ref, q_ref, kt_ref, v_ref, o_ref,
                 qq_ref, m_ref, acc_ref, s_ref, p_ref, al_ref, *, tq, n_full, n_pairs):
    nq = q_ref.shape[0] // tq
    lane = lax.broadcasted_iota(jnp.int32, (tq, LANES), 1)
    for i in range(nq):
        q = q_ref[i * tq:(i + 1) * tq, :]
        zero = jnp.zeros_like(q)
        qq_ref[2 * i * tq:(2 * i + 1) * tq, :] = jnp.where(lane < HEAD_DIM, q, zero)
        qq_ref[(2 * i + 1) * tq:(2 * i + 2) * tq, :] = jnp.where(lane >= HEAD_DIM, q, zero)
    m_ref[...] = jnp.full(m_ref.shape, M_FLOOR, F32)
    acc_ref[...] = jnp.zeros(acc_ref.shape, F32)
    s_ref[...] = jnp.full(s_ref.shape, -jnp.inf, F32)
    p_ref[...] = jnp.zeros(p_ref.shape, BF16)
    al_ref[...] = jnp.ones(al_ref.shape, F32)

    def stage_pv(i, t):
        row0 = pl.multiple_of(i * 2 * tq, 2 * tq)
        col0 = pl.multiple_of(t * tq, tq)
        v_t = v_ref[pl.ds(col0, tq), :]
        v_e = jnp.concatenate([v_t, jnp.ones_like(v_t)], axis=1)
        pv = jnp.dot(p_ref[...], v_e, preferred_element_type=F32)
        al = al_ref[...]
        al2 = jnp.concatenate([al, al], axis=1)
        acc_ref[pl.ds(row0, 2 * tq), :] = al2 * acc_ref[pl.ds(row0, 2 * tq), :] + pv

    def stage_sm(i):
        row0 = pl.multiple_of(i * 2 * tq, 2 * tq)
        r = SOFTMAX_ROWS
        for c in range(2 * tq // r):
            rows = slice(c * r, (c + 1) * r)
            s = s_ref[rows, :]
            m_prev = m_ref[pl.ds(row0 + c * r, r), :]
            m_new = jnp.maximum(m_prev, jnp.max(s, axis=1, keepdims=True))
            al_ref[rows, :] = jnp.exp(m_prev - m_new)
            p_ref[rows, :] = jnp.exp(s - jnp.tile(m_new, (1, tq // LANES))).astype(BF16)
            m_ref[pl.ds(row0 + c * r, r), :] = m_new

    def stage_qk(i, t, masked):
        row0 = pl.multiple_of(i * 2 * tq, 2 * tq)
        col0 = pl.multiple_of(t * tq, tq)
        s = jnp.dot(qq_ref[pl.ds(row0, 2 * tq), :], kt_ref[:, pl.ds(col0, tq)], preferred_element_type=F32)
        if masked:
            rr = lax.broadcasted_iota(jnp.int32, (2 * tq, tq), 0)
            cc = lax.broadcasted_iota(jnp.int32, (2 * tq, tq), 1)
            rr = jnp.where(rr >= tq, rr - tq, rr)
            s = jnp.where(cc <= rr, s, -jnp.inf)
        s_ref[...] = s

    def pair(it):
        it = jnp.clip(it, 0, n_pairs - 1)
        return tab_ref[it], tab_ref[n_pairs + it]

    def trip(it, masked):
        i2, t2 = pair(it - 2)
        i1, _ = pair(it - 1)
        i0, t0 = pair(it)
        stage_pv(i2, t2)
        stage_sm(i1)
        stage_qk(i0, t0, masked)

    def full_body(it, carry):
        trip(it, False)
        return carry

    def diag_body(it, carry):
        trip(it, True)
        return carry

    if n_full:
        lax.fori_loop(0, n_full, full_body, 0)
    lax.fori_loop(n_full, n_pairs, diag_body, 0)
    i_p, t_p = pair(n_pairs - 2)
    i_l, t_l = pair(n_pairs - 1)
    stage_pv(i_p, t_p)
    stage_sm(i_l)
    stage_pv(i_l, t_l)

    lam = _lam_value(lamv_ref)
    gain = gain_ref[...]
    for i in range(nq):
        a1 = acc_ref[2 * i * tq:(2 * i + 1) * tq, :]
        a2 = acc_ref[(2 * i + 1) * tq:(2 * i + 2) * tq, :]
        o = a1[:, :V_DIM] / a1[:, V_DIM:] - lam * (a2[:, :V_DIM] / a2[:, V_DIM:])
        o_ref[i * tq:(i + 1) * tq, :] = _subln(o, gain)


def _attn_prompt(lamv, gain, qb, kbt, vb, tq):
    b, L, _ = qb.shape
    nq = L // tq
    pairs = [(i, t) for i in range(nq) for t in range(i)] + [(i, i) for i in range(nq)]
    n_pairs = len(pairs)
    n_full = n_pairs - nq
    tab = jnp.asarray(np.array([p[0] for p in pairs] + [p[1] for p in pairs], np.int32))
    grid_spec = pltpu.PrefetchScalarGridSpec(
        num_scalar_prefetch=1,
        grid=(b, N_HEADS),
        in_specs=[
            pl.BlockSpec((4, HEAD_DIM), lambda bi, h, tab: (0, 0)),
            pl.BlockSpec((1, V_DIM), lambda bi, h, tab: (0, 0)),
            pl.BlockSpec((None, L, LANES), lambda bi, h, tab: (bi, 0, h)),
            pl.BlockSpec((None, None, LANES, L), lambda bi, h, tab: (bi, h, 0, 0)),
            pl.BlockSpec((None, L, LANES), lambda bi, h, tab: (bi, 0, h)),
        ],
        out_specs=pl.BlockSpec((None, L, LANES), lambda bi, h, tab: (bi, 0, h)),
        scratch_shapes=[
            pltpu.VMEM((2 * L, LANES), BF16),
            pltpu.VMEM((2 * L, LANES), F32),
            pltpu.VMEM((2 * L, 2 * V_DIM), F32),
            pltpu.VMEM((2 * tq, tq), F32),
            pltpu.VMEM((2 * tq, tq), BF16),
            pltpu.VMEM((2 * tq, LANES), F32),
        ],
    )
    return pl.pallas_call(
        functools.partial(_attn_kernel, tq=tq, n_full=n_full, n_pairs=n_pairs),
        out_shape=jax.ShapeDtypeStruct((b, L, V_WIDTH), F32),
        grid_spec=grid_spec,
        compiler_params=_cparams(("parallel", "parallel")),
        name="attn_prompt",
    )(tab, lamv, gain, qb, kbt, vb)


def _decode_kernel(pt_ref, lamv_ref, gain_ref, q_ref, kn_ref, vn_ref, *rest):
    g = (len(rest) - 5) // 2
    k_refs = rest[:g]
    v_refs = rest[g:2 * g]
    o_ref = rest[2 * g]
    qbd_ref, m_ref, l_ref, acc_ref = rest[2 * g + 1:]
    n_maps = 2 * N_HEADS
    j = pl.program_id(1)

    @pl.when(j == 0)
    def _():
        row = lax.broadcasted_iota(jnp.int32, (n_maps, QK_WIDTH), 0)
        col = lax.broadcasted_iota(jnp.int32, (n_maps, QK_WIDTH), 1)
        qrow = jnp.broadcast_to(q_ref[...].astype(F32), (n_maps, QK_WIDTH))
        qbd_ref[...] = jnp.where(col // HEAD_DIM == row, qrow, 0.0).astype(BF16)
        m_ref[...] = jnp.full(m_ref.shape, -jnp.inf, F32)
        l_ref[...] = jnp.zeros(l_ref.shape, F32)
        acc_ref[...] = jnp.zeros(acc_ref.shape, F32)

    qbd = qbd_ref[...]
    m = m_ref[...]
    s = jnp.concatenate([jnp.dot(qbd, k_refs[t][...].astype(BF16), preferred_element_type=F32) for t in range(g)],
                        axis=1)
    m_new = jnp.maximum(m, jnp.max(s, axis=1, keepdims=True))
    alpha = jnp.exp(m - m_new)
    p = jnp.exp(s - m_new).astype(BF16)
    l = alpha * l_ref[...] + jnp.sum(p.astype(F32), axis=1, keepdims=True)
    pv = []
    for h in range(N_HEADS):
        vh = jnp.concatenate([v_refs[t][pl.ds(h, PAGE_SIZE, stride=N_HEADS), :].astype(BF16) for t in range(g)],
                             axis=0)
        pv.append(jnp.dot(p, vh, preferred_element_type=F32))
    acc = alpha * acc_ref[...] + jnp.concatenate(pv, axis=1)
    m_ref[...] = m_new
    l_ref[...] = l
    acc_ref[...] = acc

    @pl.when(j == pl.num_programs(1) - 1)
    def _():
        kn = kn_ref[...].astype(F32)
        vn = vn_ref[...].astype(F32)
        s_self = jnp.sum(qbd.astype(F32) * kn, axis=1, keepdims=True)
        m_fin = jnp.maximum(m_new, s_self)
        a_fin = jnp.exp(m_new - m_fin)
        p_self = jnp.exp(s_self - m_fin)
        lf = a_fin * l + p_self
        accf = a_fin * acc + p_self.astype(BF16).astype(F32) * vn
        o = accf / lf
        lam = _lam_value(lamv_ref)
        gain = gain_ref[...]
        for h in range(N_HEADS):
            sl = slice(h * V_DIM, (h + 1) * V_DIM)
            oh = o[2 * h:2 * h + 1, sl] - lam * o[2 * h + 1:2 * h + 2, sl]
            o_ref[:, sl] = _subln(oh, gain)


def _attn_sample(page_table, lamv, gain, qb, k_new, v_new, cache_k, cache_v):
    db, n_pages = page_table.shape
    g = min(PAGES_PER_STEP, n_pages)
    n_maps = 2 * N_HEADS
    row_spec = pl.BlockSpec((None, 1, QK_WIDTH), lambda b, j, pt: (b, 0, 0))

    def page_spec(t):
        return pl.BlockSpec((None, QK_WIDTH, PAGE_SIZE), lambda b, j, pt: (pt[b, j * g + t], 0, 0))

    grid_spec = pltpu.PrefetchScalarGridSpec(
        num_scalar_prefetch=1,
        grid=(db, n_pages // g),
        in_specs=[
            pl.BlockSpec((4, HEAD_DIM), lambda b, j, pt: (0, 0)),
            pl.BlockSpec((1, V_DIM), lambda b, j, pt: (0, 0)),
            row_spec, row_spec, row_spec,
        ] + [page_spec(t) for t in range(g)] + [page_spec(t) for t in range(g)],
        out_specs=row_spec,
        scratch_shapes=[
            pltpu.VMEM((n_maps, QK_WIDTH), BF16),
            pltpu.VMEM((n_maps, 1), F32),
            pltpu.VMEM((n_maps, 1), F32),
            pltpu.VMEM((n_maps, V_WIDTH), F32),
        ],
    )
    out = pl.pallas_call(
        _decode_kernel,
        out_shape=jax.ShapeDtypeStruct((db, 1, V_WIDTH), F32),
        grid_spec=grid_spec,
        compiler_params=_cparams(("parallel", "arbitrary")),
        name="attn_sample",
    )(page_table, lamv, gain, qb.reshape(db, 1, QK_WIDTH), k_new.reshape(db, 1, QK_WIDTH),
      v_new.reshape(db, 1, V_WIDTH), *([cache_k] * g), *([cache_v] * g))
    return out.reshape(db, V_WIDTH)


def _merge_kernel(*refs, prompt, tiles_per_seq):
    if prompt:
        (h_ref, u_ref, halo_ref, yb_ref, ga_ref, gb_ref, pw_ref, ps_ref, wo_ref, nf_ref, wr_ref, br_ref,
         h2_ref, xn_ref, ti_ref, tw_ref, ext_ref) = refs
    else:
        (h_ref, u_ref, st_ref, yb_ref, ga_ref, gb_ref, pw_ref, ps_ref, wo_ref, nf_ref, wr_ref, br_ref,
         h2_ref, xn_ref, ti_ref, tw_ref) = refs
    tm = h_ref.shape[0]
    u = u_ref[...]

    if prompt:
        seq_tile = pl.program_id(0) % tiles_per_seq
        halo = jnp.where(seq_tile == 0, 0.0, halo_ref[...])
        ext_ref[0:POOL_HALO, :] = halo
        ext_ref[POOL_HALO:, :] = u
        pos = seq_tile * tm + lax.broadcasted_iota(jnp.int32, (tm, 1), 0)

    ya_parts = []
    for gi, w in enumerate(POOL_WINDOWS):
        sl = slice(gi * POOL_GROUP_DIM, (gi + 1) * POOL_GROUP_DIM)
        cur = u[:, sl]
        wsum = cur
        if prompt:
            for d in range(1, w):
                wsum = wsum + ext_ref[POOL_HALO - d:POOL_HALO - d + tm, sl]
            cnt = jnp.minimum(pos + 1, w).astype(F32)
        else:
            for d in range(1, w):
                wsum = wsum + st_ref[POOL_STATE_LEN - d][:, sl]
            cnt = float(w)
        pooled = wsum / cnt - cur
        ya_parts.append(jnp.dot(pooled.astype(BF16), pw_ref[gi], preferred_element_type=F32))
    ya = jnp.concatenate(ya_parts, axis=1) * ps_ref[...]

    mix = ga_ref[...] * ya + gb_ref[...] * yb_ref[...]
    h2 = h_ref[...] + jnp.dot(mix.astype(BF16), wo_ref[...], preferred_element_type=F32)
    h2_ref[...] = h2
    xn = _rms(h2) * nf_ref[...]
    for c in range(ROW_CHUNKS):
        xn_ref[pl.ds(c, tm, stride=ROW_CHUNKS), :] = xn[:, c * LANES:(c + 1) * LANES]

    wr = wr_ref[...]
    wr_hi = wr.astype(BF16)
    wr_lo = (wr - wr_hi.astype(F32)).astype(BF16)
    xn_hi = xn.astype(BF16)
    xn_lo = (xn - xn_hi.astype(F32)).astype(BF16)
    logits = (jnp.dot(xn_hi, wr_hi, preferred_element_type=F32)
              + jnp.dot(xn_lo, wr_hi, preferred_element_type=F32)
              + jnp.dot(xn_hi, wr_lo, preferred_element_type=F32)) + br_ref[...]

    lane = lax.broadcasted_iota(jnp.int32, (tm, N_EXPERTS), 1).astype(F32)
    slot = lax.broadcasted_iota(jnp.int32, (tm, TOP_K), 1)
    vals = logits
    top_i = jnp.zeros((tm, TOP_K), F32)
    top_v = jnp.zeros((tm, TOP_K), F32)
    for kk in range(TOP_K):
        mx = jnp.max(vals, axis=1, keepdims=True)
        am = jnp.min(jnp.where(vals == mx, lane, float(N_EXPERTS)), axis=1, keepdims=True)
        top_i = jnp.where(slot == kk, am, top_i)
        top_v = jnp.where(slot == kk, mx, top_v)
        vals = jnp.where(lane == am, -jnp.inf, vals)
    e = jnp.exp(top_v - jnp.max(top_v, axis=1, keepdims=True))
    ti_ref[...] = top_i.astype(jnp.int32)
    tw_ref[...] = e / jnp.sum(e, axis=1, keepdims=True)


def _merge(h, u, hist, yb, ga, gb, pool_w_bf, pool_scale, w_out_bf, norm_ffn, w_router, b_router, tm, prompt,
           tiles_per_seq):
    n = h.shape[0]
    row = lambda i: (i, 0)
    const2 = lambda i: (0, 0)
    if prompt:
        per = tm // POOL_HALO
        hist_spec = pl.BlockSpec((POOL_HALO, POOL_WIDTH), lambda i: (jnp.maximum(i * per - 1, 0), 0))
        scratch = [pltpu.VMEM((POOL_HALO + tm, POOL_WIDTH), F32)]
    else:
        hist_spec = pl.BlockSpec((POOL_STATE_LEN, tm, POOL_WIDTH), lambda i: (0, i, 0))
        scratch = []
    outs = [
        jax.ShapeDtypeStruct((n, D_MODEL), F32),
        jax.ShapeDtypeStruct((n * ROW_CHUNKS, LANES), F32),
        jax.ShapeDtypeStruct((n, TOP_K), jnp.int32),
        jax.ShapeDtypeStruct((n, TOP_K), F32),
    ]
    out_specs = [pl.BlockSpec((tm, o.shape[1]), row) for o in outs]
    out_specs[1] = pl.BlockSpec((tm * ROW_CHUNKS, LANES), row)
    return pl.pallas_call(
        functools.partial(_merge_kernel, prompt=prompt, tiles_per_seq=tiles_per_seq),
        out_shape=outs,
        grid=(n // tm,),
        in_specs=[
            pl.BlockSpec((tm, D_MODEL), row),
            pl.BlockSpec((tm, POOL_WIDTH), row),
            hist_spec,
            pl.BlockSpec((tm, D_MODEL), row),
            pl.BlockSpec((tm, D_MODEL), row),
            pl.BlockSpec((tm, D_MODEL), row),
            pl.BlockSpec(pool_w_bf.shape, lambda i: (0, 0, 0)),
            pl.BlockSpec((1, D_MODEL), const2),
            pl.BlockSpec((D_MODEL, D_MODEL), const2),
            pl.BlockSpec((1, D_MODEL), const2),
            pl.BlockSpec((D_MODEL, N_EXPERTS), const2),
            pl.BlockSpec((1, N_EXPERTS), const2),
        ],
        out_specs=out_specs,
        scratch_shapes=scratch,
        compiler_params=_cparams(("parallel",)),
        name="merge_prompt" if prompt else "merge_sample",
    )(h, u, hist, yb, ga, gb, pool_w_bf, pool_scale.reshape(1, D_MODEL), w_out_bf,
      norm_ffn.reshape(1, D_MODEL), w_router, b_router.reshape(1, N_EXPERTS))


def _row_tile(ref, row):
    return ref.at[pl.ds(pl.multiple_of(row * ROW_CHUNKS, ROW_CHUNKS), ROW_CHUNKS), :]


def _dispatch_kernel(zs_ref, dest_ref, xp_ref, xs_ref, o_hbm, zero_ref, zsem, sem, *, np_tiles, bm):
    i = pl.program_id(0)
    tn = xp_ref.shape[0] // ROW_CHUNKS

    def zero_copy(e):
        start = pl.multiple_of(zs_ref[e] * ROW_CHUNKS, ROW_CHUNKS)
        return pltpu.make_async_copy(zero_ref, o_hbm.at[pl.ds(start, bm * ROW_CHUNKS), :], zsem)

    @pl.when(i == 0)
    def _():
        zero_ref[...] = jnp.zeros(zero_ref.shape, F32)
        for e in range(zs_ref.shape[0]):
            @pl.when(zs_ref[e] >= 0)
            def _():
                zero_copy(e).start()
        for e in range(zs_ref.shape[0]):
            @pl.when(zs_ref[e] >= 0)
            def _():
                zero_copy(e).wait()

    def scatter(src_ref):
        def issue(t, carry):
            for kk in range(TOP_K):
                pltpu.make_async_copy(_row_tile(src_ref, t), _row_tile(o_hbm, dest_ref[0, t * TOP_K + kk]), sem).start()
            return carry

        lax.fori_loop(0, tn, issue, 0, unroll=4)
        for kk in range(TOP_K):
            pltpu.make_async_copy(src_ref, src_ref, sem).wait()

    @pl.when(i < np_tiles)
    def _():
        scatter(xp_ref)

    @pl.when(i >= np_tiles)
    def _():
        scatter(xs_ref)


def _dispatch(zero_starts, dest, xn_p, xn_s, rows, tn, bm):
    np_tiles = xn_p.shape[0] // (tn * ROW_CHUNKS)
    ns_tiles = xn_s.shape[0] // (tn * ROW_CHUNKS)
    steps = np_tiles + ns_tiles
    grid_spec = pltpu.PrefetchScalarGridSpec(
        num_scalar_prefetch=1,
        grid=(steps,),
        in_specs=[
            pl.BlockSpec((None, 1, tn * TOP_K), lambda i, zs: (i, 0, 0), memory_space=pltpu.SMEM),
            pl.BlockSpec((tn * ROW_CHUNKS, LANES), lambda i, zs: (jnp.minimum(i, np_tiles - 1), 0)),
            pl.BlockSpec((tn * ROW_CHUNKS, LANES), lambda i, zs: (jnp.maximum(i - np_tiles, 0), 0)),
        ],
        out_specs=pl.BlockSpec(memory_space=pl.ANY),
        scratch_shapes=[pltpu.VMEM((bm * ROW_CHUNKS, LANES), F32), pltpu.SemaphoreType.DMA, pltpu.SemaphoreType.DMA],
    )
    return pl.pallas_call(
        functools.partial(_dispatch_kernel, np_tiles=np_tiles, bm=bm),
        out_shape=jax.ShapeDtypeStruct((rows * ROW_CHUNKS, LANES), F32),
        grid_spec=grid_spec,
        compiler_params=_cparams(("arbitrary",)),
        name="moe_dispatch",
    )(zero_starts, dest.reshape(steps, 1, tn * TOP_K), xn_p, xn_s)


def _ffn_kernel(be_ref, nu_ref, x_ref, wg_ref, bg_ref, wu_ref, bu_ref, wd_ref, bd_ref, y_ref):
    i = pl.program_id(0)
    bm = x_ref.shape[0] // ROW_CHUNKS

    @pl.when(i < nu_ref[0])
    def _():
        xb = jnp.concatenate([_chunk_rows(x_ref, c, bm).astype(BF16) for c in range(ROW_CHUNKS)], axis=1)
        g = jnp.dot(xb, wg_ref[...], preferred_element_type=F32) + bg_ref[...]
        u = jnp.dot(xb, wu_ref[...], preferred_element_type=F32) + bu_ref[...]
        g = jnp.minimum(g, SWIGLU_LIMIT)
        u = jnp.clip(u, -SWIGLU_LIMIT, SWIGLU_LIMIT)
        act = g * jax.nn.sigmoid(SWIGLU_ALPHA * g) * (u + 1.0)
        y = jnp.dot(act.astype(BF16), wd_ref[...], preferred_element_type=F32) + bd_ref[...]
        for c in range(ROW_CHUNKS):
            y_ref[pl.ds(c, bm, stride=ROW_CHUNKS), :] = y[:, c * LANES:(c + 1) * LANES]

    @pl.when(i >= nu_ref[0])
    def _():
        y_ref[...] = jnp.zeros(y_ref.shape, F32)


def _ffn(blk_e, n_used, xs, wg, bg, wu, bu, wd, bd, bm):
    n_blocks = xs.shape[0] // (bm * ROW_CHUNKS)
    w_spec = pl.BlockSpec((None, D_MODEL, D_MODEL), lambda i, be, nu: (be[i], 0, 0))
    b_spec = pl.BlockSpec((None, 1, D_MODEL), lambda i, be, nu: (be[i], 0, 0))
    x_spec = pl.BlockSpec((bm * ROW_CHUNKS, LANES), lambda i, be, nu: (jnp.minimum(i, nu[0] - 1), 0))
    y_spec = pl.BlockSpec((bm * ROW_CHUNKS, LANES), lambda i, be, nu: (i, 0))
    grid_spec = pltpu.PrefetchScalarGridSpec(
        num_scalar_prefetch=2,
        grid=(n_blocks,),
        in_specs=[x_spec, w_spec, b_spec, w_spec, b_spec, w_spec, b_spec],
        out_specs=y_spec,
    )
    e = bg.shape[0]
    return pl.pallas_call(
        _ffn_kernel,
        out_shape=jax.ShapeDtypeStruct(xs.shape, F32),
        grid_spec=grid_spec,
        compiler_params=_cparams(("arbitrary",)),
        name="moe_ffn",
    )(blk_e, n_used, xs, wg, bg.reshape(e, 1, D_MODEL), wu, bu.reshape(e, 1, D_MODEL), wd, bd.reshape(e, 1, D_MODEL))


def _combine_kernel(dest_ref, hp_ref, hs_ref, wp_ref, ws_ref, nf_ref, ys_hbm, yp_ref, ysm_ref, buf_ref, sem,
                    *, np_tiles):
    i = pl.program_id(0)
    tn = hp_ref.shape[0]

    def issue(t, carry):
        for kk in range(TOP_K):
            pltpu.make_async_copy(_row_tile(ys_hbm, dest_ref[0, t * TOP_K + kk]), _row_tile(buf_ref.at[kk], t), sem).start()
        return carry

    lax.fori_loop(0, tn, issue, 0, unroll=4)
    for kk in range(TOP_K):
        pltpu.make_async_copy(buf_ref.at[kk], buf_ref.at[kk], sem).wait()

    def finish(h_ref, w_ref, out_ref):
        w = w_ref[...]
        h = h_ref[...]
        cols = []
        for c in range(ROW_CHUNKS):
            acc = h[:, c * LANES:(c + 1) * LANES]
            for kk in range(TOP_K):
                acc = acc + w[:, kk:kk + 1] * _chunk_rows(buf_ref.at[kk], c, tn)
            cols.append(acc)
        out_ref[...] = _rms(jnp.concatenate(cols, axis=1)) * nf_ref[...]

    @pl.when(i < np_tiles)
    def _():
        finish(hp_ref, wp_ref, yp_ref)

    @pl.when(i >= np_tiles)
    def _():
        finish(hs_ref, ws_ref, ysm_ref)


def _combine(dest, h_p, h_s, w_p, w_s, norm_final, ys, tn):
    np_tiles = h_p.shape[0] // tn
    ns_tiles = h_s.shape[0] // tn
    steps = np_tiles + ns_tiles
    p_map = lambda i: (jnp.minimum(i, np_tiles - 1), 0)
    s_map = lambda i: (jnp.maximum(i - np_tiles, 0), 0)
    return pl.pallas_call(
        functools.partial(_combine_kernel, np_tiles=np_tiles),
        out_shape=[jax.ShapeDtypeStruct(h_p.shape, F32), jax.ShapeDtypeStruct(h_s.shape, F32)],
        grid=(steps,),
        in_specs=[
            pl.BlockSpec((None, 1, tn * TOP_K), lambda i: (i, 0, 0), memory_space=pltpu.SMEM),
            pl.BlockSpec((tn, D_MODEL), p_map),
            pl.BlockSpec((tn, D_MODEL), s_map),
            pl.BlockSpec((tn, TOP_K), p_map),
            pl.BlockSpec((tn, TOP_K), s_map),
            pl.BlockSpec((1, D_MODEL), lambda i: (0, 0)),
            pl.BlockSpec(memory_space=pl.ANY),
        ],
        out_specs=[pl.BlockSpec((tn, D_MODEL), p_map), pl.BlockSpec((tn, D_MODEL), s_map)],
        scratch_shapes=[pltpu.VMEM((TOP_K, tn * ROW_CHUNKS, LANES), F32), pltpu.SemaphoreType.DMA],
        compiler_params=_cparams(("arbitrary",)),
        name="moe_combine",
    )(dest.reshape(steps, 1, tn * TOP_K), h_p, h_s, w_p, w_s, norm_final.reshape(1, D_MODEL), ys)


def _moe_final(h_p, xn_p, ti_p, tw_p, h_s, xn_s, ti_s, tw_s, wg, bg, wu, bu, wd, bd, norm_final):
    n_p, n_s = h_p.shape[0], h_s.shape[0]
    n = n_p + n_s
    nk = n * TOP_K
    bm = MOE_BLOCK
    tn = min(MOE_TOKENS, n_s)
    top_i = jnp.concatenate([ti_p, ti_s], axis=0)
    sel = (top_i[:, :, None] == jnp.arange(N_EXPERTS, dtype=jnp.int32)[None, None, :]).astype(jnp.int32).sum(1)
    incl = jnp.cumsum(sel, axis=0)
    counts = incl[-1]
    rank = jnp.take_along_axis(incl - sel, top_i, axis=1)
    padded = (counts + bm - 1) // bm * bm
    pad_end = jnp.cumsum(padded)
    start_pad = pad_end - padded
    dest = (start_pad[top_i] + rank).astype(jnp.int32)
    n_blocks = -(-nk // bm) + N_EXPERTS
    rows = n_blocks * bm
    blk_start = jnp.arange(n_blocks, dtype=jnp.int32) * bm
    blk_e = jnp.minimum(jnp.sum((pad_end[None, :] <= blk_start[:, None]).astype(jnp.int32), axis=1),
                        N_EXPERTS - 1).astype(jnp.int32)
    n_used = (pad_end[-1] // bm).astype(jnp.int32).reshape(1)
    tail_blk = n_blocks - 1 - jnp.arange(N_EXPERTS, dtype=jnp.int32)
    zero_starts = jnp.concatenate([jnp.where(padded > 0, pad_end - bm, -1),
                                   jnp.where(tail_blk >= n_used[0], tail_blk * bm, -1)]).astype(jnp.int32)

    xs = _dispatch(zero_starts, dest, xn_p, xn_s, rows, tn, bm)
    ys = _ffn(blk_e, n_used, xs, wg, bg, wu, bu, wd, bd, bm)
    return _combine(dest, h_p, h_s, tw_p, tw_s, norm_final, ys, tn)


def kernel(x_prompt, x_sample, cache_k, cache_v, state_pool, page_table, norm_mix, w_in, pool_w, pool_scale,
           lambda_q1, lambda_k1, lambda_q2, lambda_k2, subln_gain, w_out, norm_ffn, w_router, b_router,
           w_gate, b_gate, w_up, b_up, w_down, b_down, norm_final):
    bp, lp, d = x_prompt.shape
    db = x_sample.shape[0]
    layer = 0
    n_pool = cache_k.shape[1]

    w_in_bf = w_in[layer].astype(BF16)
    w_out_bf = w_out[layer].astype(BF16)
    pool_w_bf = pool_w[layer].astype(BF16)
    wg = w_gate[layer].astype(BF16)
    wu = w_up[layer].astype(BF16)
    wd = w_down[layer].astype(BF16)
    lamv = jnp.stack([lambda_q1[layer], lambda_k1[layer], lambda_q2[layer], lambda_k2[layer]]).astype(F32)
    gain = subln_gain[layer].reshape(1, V_DIM)

    cos_s, sin_s = _rope_tables(jnp.full((db,), PAST_LEN))
    xs = x_sample.reshape(db, d)
    us, qbs, kts, kbs, vs, vbs, gas, gbs = _proj(xs, norm_mix[layer], w_in_bf, cos_s, sin_s, db, None)
    cache_kt = jnp.transpose(cache_k[layer], (0, 2, 3, 4, 1)).reshape(n_pool, QK_WIDTH, PAGE_SIZE)
    cache_vr = cache_v[layer].reshape(n_pool, PAGE_SIZE * N_HEADS, V_DIM)
    ybs = _attn_sample(page_table, lamv, gain, qbs, kbs, vbs, cache_kt, cache_vr)
    state_t = jnp.transpose(state_pool[layer], (1, 0, 2))
    h2s, xns, tis, tws = _merge(xs, us, state_t, ybs, gas, gbs, pool_w_bf, pool_scale[layer], w_out_bf,
                                norm_ffn[layer], w_router[layer], b_router[layer], db, False, 1)
    k_sample = jnp.transpose(kts.reshape(QK_WIDTH, db), (1, 0)).reshape(1, db, 1, N_HEADS, 2, HEAD_DIM)
    v_sample = vs.reshape(1, db, 1, N_HEADS, V_DIM)
    pool_sample = jnp.concatenate([state_pool[layer][:, 1:, :], us[:, None, :]], axis=1)[None]

    tm = min(ROW_TILE, lp)
    tq = min(Q_TILE, lp)
    cos_p, sin_p = _rope_tables(jnp.arange(lp))
    xp = x_prompt.reshape(bp * lp, d)
    u, qb, kt, kbt, v, vb, ga, gb = _proj(xp, norm_mix[layer], w_in_bf, cos_p, sin_p, tm, lp)
    yb = _attn_prompt(lamv, gain, qb.reshape(bp, lp, QK_WIDTH), kbt, vb.reshape(bp, lp, V_WIDTH), tq)
    h2, xn, top_i, top_w = _merge(xp, u, u, yb.reshape(bp * lp, V_WIDTH), ga, gb, pool_w_bf, pool_scale[layer],
                                  w_out_bf, norm_ffn[layer], w_router[layer], b_router[layer], tm, True, lp // tm)
    pg = kt.shape[2]
    k_prompt = jnp.transpose(kt.reshape(bp, lp // pg, N_HEADS, 2, HEAD_DIM, pg), (0, 1, 5, 2, 3, 4))
    k_prompt = k_prompt.reshape(1, bp, lp // PAGE_SIZE, PAGE_SIZE, N_HEADS, 2, HEAD_DIM)
    v_prompt = v.reshape(1, bp, lp // PAGE_SIZE, PAGE_SIZE, N_HEADS, V_DIM)
    pool_prompt = u.reshape(bp, lp, POOL_WIDTH)[:, lp - POOL_STATE_LEN:, :][None]

    y_p, y_s = _moe_final(h2, xn, top_i, top_w, h2s, xns, tis, tws,
                          wg, b_gate[layer], wu, b_up[layer], wd, b_down[layer], norm_final)
    y_prompt = y_p.reshape(bp, lp, d)
    y_sample = y_s.reshape(db, 1, d)

    return (y_prompt, y_sample, k_prompt, v_prompt, pool_prompt, k_sample, v_sample, pool_sample)
```

```python
import functools
import math

import numpy as np
import jax
import jax.numpy as jnp
from jax import lax
from jax.experimental import pallas as pl
from jax.experimental.pallas import tpu as pltpu

F32 = jnp.float32
BF16 = jnp.bfloat16

N_HEADS = 8
HEAD_DIM = 64
V_DIM = 128
D_MODEL = 1024
POOL_WIDTH = 512
POOL_WINDOWS = (2, 4, 8, 16)
POOL_GROUP_DIM = 128
POOL_OUT_DIM = 256
POOL_STATE_LEN = 15
POOL_HALO = 16
QK_WIDTH = 1024
V_WIDTH = 1024
N_EXPERTS = 32
TOP_K = 4
SWIGLU_LIMIT = 7.0
SWIGLU_ALPHA = 1.702
ROPE_THETA = 10000.0
RMS_EPS = 1e-6
PAST_LEN = 8192
PAGE_SIZE = 128
LAM_INIT = 0.8 - 0.6 * math.exp(-0.3 * 0)

LANES = 128
SUBLANES = 8
ROW_CHUNKS = D_MODEL // LANES
VMEM_LIMIT = 56 * 2**20

ROW_TILE = 256
Q_TILE = 512
SOFTMAX_ROWS = 32
PAGES_PER_STEP = 8
MOE_BLOCK = 512
MOE_TOKENS = 128
M_FLOOR = -1e30


def _cparams(semantics):
    return pltpu.CompilerParams(dimension_semantics=semantics, vmem_limit_bytes=VMEM_LIMIT)


def _rms(x, eps=RMS_EPS):
    return x * lax.rsqrt(jnp.mean(x * x, axis=-1, keepdims=True) + eps)


def _chunk_rows(ref, c, n):
    return ref[pl.ds(c, n, stride=ROW_CHUNKS), :]


def _proj_kernel(x_ref, g_ref, w_ref, cos_ref, sin_ref,
                 u_ref, q_ref, kt_ref, kb_ref, v_ref, vb_ref, ga_ref, gb_ref, *, kb_transposed):
    tm = x_ref.shape[0]
    xn = (_rms(x_ref[...]) * g_ref[...]).astype(BF16)

    def sec(a, b):
        return jnp.dot(xn, w_ref[:, a:b], preferred_element_type=F32)

    a1 = POOL_WIDTH
    a2 = a1 + QK_WIDTH
    a3 = a2 + QK_WIDTH
    a4 = a3 + V_WIDTH
    a5 = a4 + D_MODEL
    a6 = a5 + D_MODEL

    u_ref[...] = sec(0, a1)

    cos = cos_ref[...]
    sin = sin_ref[...]
    lane = lax.broadcasted_iota(jnp.int32, (tm, LANES), 1)
    first_half = (lane % HEAD_DIM) < (HEAD_DIM // 2)

    def rope_chunk(zc):
        rot = jnp.where(first_half, pltpu.roll(zc, LANES - HEAD_DIM // 2, 1), pltpu.roll(zc, HEAD_DIM // 2, 1))
        return zc * cos + rot * sin

    zq = sec(a1, a2)
    for c in range(QK_WIDTH // LANES):
        sl = slice(c * LANES, (c + 1) * LANES)
        q_ref[:, sl] = (rope_chunk(zq[:, sl]) * (HEAD_DIM ** -0.5)).astype(BF16)
    zk = sec(a2, a3)
    pg = kt_ref.shape[2]
    for c in range(QK_WIDTH // LANES):
        sl = slice(c * LANES, (c + 1) * LANES)
        r = rope_chunk(zk[:, sl])
        if not kb_transposed:
            kb_ref[:, sl] = r.astype(BF16)
        for p in range(tm // pg):
            rt = r[p * pg:(p + 1) * pg, :].T
            kt_ref[p, sl, :] = rt
            if kb_transposed:
                kb_ref[c, :, p * pg:(p + 1) * pg] = rt.astype(BF16)
    zv = sec(a3, a4)
    vb_ref[...] = zv.astype(BF16)
    for h in range(N_HEADS):
        v_ref[pl.ds(h, tm, stride=N_HEADS), :] = zv[:, h * V_DIM:(h + 1) * V_DIM]
    ga_ref[...] = jax.nn.sigmoid(sec(a4, a5))
    gb_ref[...] = jax.nn.sigmoid(sec(a5, a6))


def _rope_tables(pos):
    inv = jnp.power(ROPE_THETA, -jnp.arange(0, HEAD_DIM, 2, dtype=F32) / HEAD_DIM)
    ang = pos.astype(F32)[:, None] * inv[None, :]
    cos = jnp.concatenate([jnp.cos(ang)] * 4, -1)
    sin = jnp.concatenate([-jnp.sin(ang), jnp.sin(ang)] * 2, -1)
    return cos, sin


def _proj(x2d, norm_g, w_in_bf, cos, sin, tm, seq_len):
    n = x2d.shape[0]
    n_pos_tiles = cos.shape[0] // tm
    row = lambda i: (i, 0)
    const = lambda i: (0, 0)
    in_width = w_in_bf.shape[1]
    pg = min(PAGE_SIZE, tm)
    kb_transposed = seq_len is not None
    if kb_transposed:
        tps = seq_len // tm
        kb_shape = jax.ShapeDtypeStruct((n // seq_len, N_HEADS, 2 * HEAD_DIM, seq_len), BF16)
        kb_spec = pl.BlockSpec((None, N_HEADS, 2 * HEAD_DIM, tm), lambda i: (i // tps, 0, 0, i % tps))
    else:
        kb_shape = jax.ShapeDtypeStruct((n, QK_WIDTH), BF16)
        kb_spec = pl.BlockSpec((tm, QK_WIDTH), row)
    outs = [
        jax.ShapeDtypeStruct((n, POOL_WIDTH), F32),
        jax.ShapeDtypeStruct((n, QK_WIDTH), BF16),
        jax.ShapeDtypeStruct((n // pg, QK_WIDTH, pg), F32),
        kb_shape,
        jax.ShapeDtypeStruct((n * N_HEADS, V_DIM), F32),
        jax.ShapeDtypeStruct((n, V_WIDTH), BF16),
        jax.ShapeDtypeStruct((n, D_MODEL), F32),
        jax.ShapeDtypeStruct((n, D_MODEL), F32),
    ]
    out_specs = [pl.BlockSpec((tm, o.shape[-1]), row) for o in outs]
    out_specs[2] = pl.BlockSpec((tm // pg, QK_WIDTH, pg), lambda i: (i, 0, 0))
    out_specs[3] = kb_spec
    out_specs[4] = pl.BlockSpec((tm * N_HEADS, V_DIM), row)
    return pl.pallas_call(
        functools.partial(_proj_kernel, kb_transposed=kb_transposed),
        out_shape=outs,
        grid=(n // tm,),
        in_specs=[
            pl.BlockSpec((tm, D_MODEL), row),
            pl.BlockSpec((1, D_MODEL), const),
            pl.BlockSpec((D_MODEL, in_width), const),
            pl.BlockSpec((tm, LANES), lambda i: (i % n_pos_tiles, 0)),
            pl.BlockSpec((tm, LANES), lambda i: (i % n_pos_tiles, 0)),
        ],
        out_specs=out_specs,
        compiler_params=_cparams(("parallel",)),
        name="proj",
    )(x2d, norm_g.reshape(1, D_MODEL), w_in_bf, cos, sin)


def _lam_value(lamv_ref):
    lv = lamv_ref[...]
    a = jnp.sum(lv[0:1] * lv[1:2], axis=-1, keepdims=True)
    b = jnp.sum(lv[2:3] * lv[3:4], axis=-1, keepdims=True)
    return jnp.exp(a) - jnp.exp(b) + LAM_INIT


def _subln(o, gain):
    return (_rms(o) * gain) * (1.0 - LAM_INIT)


def _attn_kernel(tab_ref, lamv_ref, gain_ref, q_ref, kt_ref, v_ref, o_ref,
                 qq_ref, m_ref, acc_ref, s_ref, p_ref, al_ref, *, tq, n_full, n_pairs):
    nq = q_ref.shape[0] // tq
    lane = lax.broadcasted_iota(jnp.int32, (tq, LANES), 1)
    for i in range(nq):
        q = q_ref[i * tq:(i + 1) * tq, :]
        zero = jnp.zeros_like(q)
        qq_ref[2 * i * tq:(2 * i + 1) * tq, :] = jnp.where(lane < HEAD_DIM, q, zero)
        qq_ref[(2 * i + 1) * tq:(2 * i + 2) * tq, :] = jnp.where(lane >= HEAD_DIM, q, zero)
    m_ref[...] = jnp.full(m_ref.shape, M_FLOOR, F32)
    acc_ref[...] = jnp.zeros(acc_ref.shape, F32)
    s_ref[...] = jnp.full(s_ref.shape, -jnp.inf, F32)
    p_ref[...] = jnp.zeros(p_ref.shape, BF16)
    al_ref[...] = jnp.ones(al_ref.shape, F32)

    def stage_pv(i, t):
        row0 = pl.multiple_of(i * 2 * tq, 2 * tq)
        col0 = pl.multiple_of(t * tq, tq)
        v_t = v_ref[pl.ds(col0, tq), :]
        v_e = jnp.concatenate([v_t, jnp.ones_like(v_t)], axis=1)
        pv = jnp.dot(p_ref[...], v_e, preferred_element_type=F32)
        al = al_ref[...]
        al2 = jnp.concatenate([al, al], axis=1)
        acc_ref[pl.ds(row0, 2 * tq), :] = al2 * acc_ref[pl.ds(row0, 2 * tq), :] + pv

    def stage_sm(i):
        row0 = pl.multiple_of(i * 2 * tq, 2 * tq)
        r = SOFTMAX_ROWS
        for c in range(2 * tq // r):
            rows = slice(c * r, (c + 1) * r)
            s = s_ref[rows, :]
            m_prev = m_ref[pl.ds(row0 + c * r, r), :]
            m_new = jnp.maximum(m_prev, jnp.max(s, axis=1, keepdims=True))
            al_ref[rows, :] = jnp.exp(m_prev - m_new)
            p_ref[rows, :] = jnp.exp(s - jnp.tile(m_new, (1, tq // LANES))).astype(BF16)
            m_ref[pl.ds(row0 + c * r, r), :] = m_new

    def stage_qk(i, t, it, masked):
        row0 = pl.multiple_of(i * 2 * tq, 2 * tq)
        col0 = pl.multiple_of(t * tq, tq)
        s = jnp.dot(qq_ref[pl.ds(row0, 2 * tq), :], kt_ref[:, pl.ds(col0, tq)], preferred_element_type=F32)
        if masked:
            rr = lax.broadcasted_iota(jnp.int32, (2 * tq, tq), 0)
            cc = lax.broadcasted_iota(jnp.int32, (2 * tq, tq), 1)
            rr = jnp.where(rr >= tq, rr - tq, rr) + jnp.where(it < n_pairs, 0, -(tq + 1))
            s = jnp.where(cc <= rr, s, -jnp.inf)
        s_ref[...] = s

    def pair(it):
        it = jnp.clip(it, 0, n_pairs - 1)
        return tab_ref[it], tab_ref[n_pairs + it]

    def trip(it, masked):
        i2, t2 = pair(it - 2)
        i1, _ = pair(it - 1)
        i0, t0 = pair(it)
        stage_pv(i2, t2)
        stage_sm(i1)
        stage_qk(i0, t0, it, masked)

    def full_body(it, carry):
        trip(it, False)
        return carry

    def diag_body(it, carry):
        trip(it, True)
        return carry

    if n_full:
        lax.fori_loop(0, n_full, full_body, 0)
    lax.fori_loop(n_full, n_pairs + 2, diag_body, 0)

    lam = _lam_value(lamv_ref)
    gain = gain_ref[...]
    for i in range(nq):
        a1 = acc_ref[2 * i * tq:(2 * i + 1) * tq, :]
        a2 = acc_ref[(2 * i + 1) * tq:(2 * i + 2) * tq, :]
        o = a1[:, :V_DIM] / a1[:, V_DIM:] - lam * (a2[:, :V_DIM] / a2[:, V_DIM:])
        o_ref[i * tq:(i + 1) * tq, :] = _subln(o, gain)


def _attn_prompt(lamv, gain, qb, kbt, vb, tq):
    b, L, _ = qb.shape
    nq = L // tq
    pairs = [(i, t) for i in range(nq) for t in range(i)] + [(i, i) for i in range(nq)]
    n_pairs = len(pairs)
    n_full = n_pairs - nq
    tab = jnp.asarray(np.array([p[0] for p in pairs] + [p[1] for p in pairs], np.int32))
    grid_spec = pltpu.PrefetchScalarGridSpec(
        num_scalar_prefetch=1,
        grid=(b, N_HEADS),
        in_specs=[
            pl.BlockSpec((4, HEAD_DIM), lambda bi, h, tab: (0, 0)),
            pl.BlockSpec((1, V_DIM), lambda bi, h, tab: (0, 0)),
            pl.BlockSpec((None, L, LANES), lambda bi, h, tab: (bi, 0, h)),
            pl.BlockSpec((None, None, LANES, L), lambda bi, h, tab: (bi, h, 0, 0)),
            pl.BlockSpec((None, L, LANES), lambda bi, h, tab: (bi, 0, h)),
        ],
        out_specs=pl.BlockSpec((None, L, LANES), lambda bi, h, tab: (bi, 0, h)),
        scratch_shapes=[
            pltpu.VMEM((2 * L, LANES), BF16),
            pltpu.VMEM((2 * L, LANES), F32),
            pltpu.VMEM((2 * L, 2 * V_DIM), F32),
            pltpu.VMEM((2 * tq, tq), F32),
            pltpu.VMEM((2 * tq, tq), BF16),
            pltpu.VMEM((2 * tq, LANES), F32),
        ],
    )
    return pl.pallas_call(
        functools.partial(_attn_kernel, tq=tq, n_full=n_full, n_pairs=n_pairs),
        out_shape=jax.ShapeDtypeStruct((b, L, V_WIDTH), F32),
        grid_spec=grid_spec,
        compiler_params=_cparams(("parallel", "parallel")),
        name="attn_prompt",
    )(tab, lamv, gain, qb, kbt, vb)


def _decode_kernel(pt_ref, lamv_ref, gain_ref, q_ref, kn_ref, vn_ref, ck_hbm, cv_hbm, o_ref,
                   kbuf_ref, vbuf_ref, sem_ref, qbd_ref, m_ref, l_ref, acc_ref):
    g = kbuf_ref.shape[1]
    n_maps = 2 * N_HEADS
    b = pl.program_id(0)
    j = pl.program_id(1)
    nj = pl.num_programs(1)
    step = b * nj + j
    slot = step % 2

    def fetch(bb, jj, sl):
        for t in range(g):
            page = pt_ref[bb, jj * g + t]
            pltpu.make_async_copy(ck_hbm.at[page], kbuf_ref.at[sl, t], sem_ref.at[0, sl]).start()
            pltpu.make_async_copy(cv_hbm.at[page], vbuf_ref.at[sl, t], sem_ref.at[1, sl]).start()

    @pl.when(step == 0)
    def _():
        fetch(0, 0, 0)

    wrap = j + 1 == nj

    @pl.when(step + 1 < pl.num_programs(0) * nj)
    def _():
        fetch(jnp.where(wrap, b + 1, b), jnp.where(wrap, 0, j + 1), 1 - slot)

    pltpu.make_async_copy(kbuf_ref.at[slot], kbuf_ref.at[slot], sem_ref.at[0, slot]).wait()
    pltpu.make_async_copy(vbuf_ref.at[slot], vbuf_ref.at[slot], sem_ref.at[1, slot]).wait()
    k_refs = [kbuf_ref.at[slot, t] for t in range(g)]
    v_refs = [vbuf_ref.at[slot, t] for t in range(g)]

    @pl.when(j == 0)
    def _():
        row = lax.broadcasted_iota(jnp.int32, (n_maps, QK_WIDTH), 0)
        col = lax.broadcasted_iota(jnp.int32, (n_maps, QK_WIDTH), 1)
        qrow = jnp.broadcast_to(q_ref[...].astype(F32), (n_maps, QK_WIDTH))
        qbd_ref[...] = jnp.where(col // HEAD_DIM == row, qrow, 0.0).astype(BF16)
        m_ref[...] = jnp.full(m_ref.shape, -jnp.inf, F32)
        l_ref[...] = jnp.zeros(l_ref.shape, F32)
        acc_ref[...] = jnp.zeros(acc_ref.shape, F32)

    qbd = qbd_ref[...]
    m = m_ref[...]
    s = jnp.concatenate([jnp.dot(qbd, k_refs[t][...].astype(BF16), preferred_element_type=F32) for t in range(g)],
                        axis=1)
    m_new = jnp.maximum(m, jnp.max(s, axis=1, keepdims=True))
    alpha = jnp.exp(m - m_new)
    p = jnp.exp(s - m_new).astype(BF16)
    l = alpha * l_ref[...] + jnp.sum(p.astype(F32), axis=1, keepdims=True)
    pv = []
    for h in range(N_HEADS):
        vh = jnp.concatenate([v_refs[t][pl.ds(h, PAGE_SIZE, stride=N_HEADS), :].astype(BF16) for t in range(g)],
                             axis=0)
        pv.append(jnp.dot(p, vh, preferred_element_type=F32))
    acc = alpha * acc_ref[...] + jnp.concatenate(pv, axis=1)
    m_ref[...] = m_new
    l_ref[...] = l
    acc_ref[...] = acc

    @pl.when(j == pl.num_programs(1) - 1)
    def _():
        kn = kn_ref[...].astype(F32)
        vn = vn_ref[...].astype(F32)
        s_self = jnp.sum(qbd.astype(F32) * kn, axis=1, keepdims=True)
        m_fin = jnp.maximum(m_new, s_self)
        a_fin = jnp.exp(m_new - m_fin)
        p_self = jnp.exp(s_self - m_fin)
        lf = a_fin * l + p_self
        accf = a_fin * acc + p_self.astype(BF16).astype(F32) * vn
        o = accf / lf
        lam = _lam_value(lamv_ref)
        gain = gain_ref[...]
        for h in range(N_HEADS):
            sl = slice(h * V_DIM, (h + 1) * V_DIM)
            oh = o[2 * h:2 * h + 1, sl] - lam * o[2 * h + 1:2 * h + 2, sl]
            o_ref[:, sl] = _subln(oh, gain)


def _attn_sample(page_table, lamv, gain, qb, k_new, v_new, cache_k, cache_v):
    db, n_pages = page_table.shape
    g = min(PAGES_PER_STEP, n_pages)
    n_maps = 2 * N_HEADS
    row_spec = pl.BlockSpec((None, 1, QK_WIDTH), lambda b, j, pt: (b, 0, 0))
    grid_spec = pltpu.PrefetchScalarGridSpec(
        num_scalar_prefetch=1,
        grid=(db, n_pages // g),
        in_specs=[
            pl.BlockSpec((4, HEAD_DIM), lambda b, j, pt: (0, 0)),
            pl.BlockSpec((1, V_DIM), lambda b, j, pt: (0, 0)),
            row_spec, row_spec, row_spec,
            pl.BlockSpec(memory_space=pl.ANY),
            pl.BlockSpec(memory_space=pl.ANY),
        ],
        out_specs=row_spec,
        scratch_shapes=[
            pltpu.VMEM((2, g, QK_WIDTH, PAGE_SIZE), F32),
            pltpu.VMEM((2, g, PAGE_SIZE * N_HEADS, V_DIM), F32),
            pltpu.SemaphoreType.DMA((2, 2)),
            pltpu.VMEM((n_maps, QK_WIDTH), BF16),
            pltpu.VMEM((n_maps, 1), F32),
            pltpu.VMEM((n_maps, 1), F32),
            pltpu.VMEM((n_maps, V_WIDTH), F32),
        ],
    )
    out = pl.pallas_call(
        _decode_kernel,
        out_shape=jax.ShapeDtypeStruct((db, 1, V_WIDTH), F32),
        grid_spec=grid_spec,
        compiler_params=_cparams(("arbitrary", "arbitrary")),
        name="attn_sample",
    )(page_table, lamv, gain, qb.reshape(db, 1, QK_WIDTH), k_new.reshape(db, 1, QK_WIDTH),
      v_new.reshape(db, 1, V_WIDTH), cache_k, cache_v)
    return out.reshape(db, V_WIDTH)


def _merge_kernel(*refs, prompt, tiles_per_seq):
    if prompt:
        (h_ref, u_ref, halo_ref, yb_ref, ga_ref, gb_ref, pw_ref, ps_ref, wo_ref, nf_ref, wr_ref, br_ref,
         h2_ref, xn_ref, ti_ref, tw_ref, ext_ref) = refs
    else:
        (h_ref, u_ref, st_ref, yb_ref, ga_ref, gb_ref, pw_ref, ps_ref, wo_ref, nf_ref, wr_ref, br_ref,
         h2_ref, xn_ref, ti_ref, tw_ref) = refs
    tm = h_ref.shape[0]
    u = u_ref[...]

    if prompt:
        seq_tile = pl.program_id(0) % tiles_per_seq
        halo = jnp.where(seq_tile == 0, 0.0, halo_ref[...])
        ext_ref[0:POOL_HALO, :] = halo
        ext_ref[POOL_HALO:, :] = u
        pos = seq_tile * tm + lax.broadcasted_iota(jnp.int32, (tm, 1), 0)

    ya_parts = []
    for gi, w in enumerate(POOL_WINDOWS):
        sl = slice(gi * POOL_GROUP_DIM, (gi + 1) * POOL_GROUP_DIM)
        cur = u[:, sl]
        wsum = cur
        if prompt:
            for d in range(1, w):
                wsum = wsum + ext_ref[POOL_HALO - d:POOL_HALO - d + tm, sl]
            cnt = jnp.minimum(pos + 1, w).astype(F32)
        else:
            for d in range(1, w):
                wsum = wsum + st_ref[POOL_STATE_LEN - d][:, sl]
            cnt = float(w)
        pooled = wsum / cnt - cur
        ya_parts.append(jnp.dot(pooled.astype(BF16), pw_ref[gi], preferred_element_type=F32))
    ya = jnp.concatenate(ya_parts, axis=1) * ps_ref[...]

    mix = ga_ref[...] * ya + gb_ref[...] * yb_ref[...]
    h2 = h_ref[...] + jnp.dot(mix.astype(BF16), wo_ref[...], preferred_element_type=F32)
    h2_ref[...] = h2
    xn = _rms(h2) * nf_ref[...]
    for c in range(ROW_CHUNKS):
        xn_ref[pl.ds(c, tm, stride=ROW_CHUNKS), :] = xn[:, c * LANES:(c + 1) * LANES]

    wr = wr_ref[...]
    wr_hi = wr.astype(BF16)
    wr_lo = (wr - wr_hi.astype(F32)).astype(BF16)
    xn_hi = xn.astype(BF16)
    xn_lo = (xn - xn_hi.astype(F32)).astype(BF16)
    logits = (jnp.dot(xn_hi, wr_hi, preferred_element_type=F32)
              + jnp.dot(xn_lo, wr_hi, preferred_element_type=F32)
              + jnp.dot(xn_hi, wr_lo, preferred_element_type=F32)) + br_ref[...]

    lane = lax.broadcasted_iota(jnp.int32, (tm, N_EXPERTS), 1).astype(F32)
    slot = lax.broadcasted_iota(jnp.int32, (tm, TOP_K), 1)
    vals = logits
    top_i = jnp.zeros((tm, TOP_K), F32)
    top_v = jnp.zeros((tm, TOP_K), F32)
    for kk in range(TOP_K):
        mx = jnp.max(vals, axis=1, keepdims=True)
        am = jnp.min(jnp.where(vals == mx, lane, float(N_EXPERTS)), axis=1, keepdims=True)
        top_i = jnp.where(slot == kk, am, top_i)
        top_v = jnp.where(slot == kk, mx, top_v)
        vals = jnp.where(lane == am, -jnp.inf, vals)
    e = jnp.exp(top_v - jnp.max(top_v, axis=1, keepdims=True))
    ti_ref[...] = top_i.astype(jnp.int32)
    tw_ref[...] = e / jnp.sum(e, axis=1, keepdims=True)


def _merge(h, u, hist, yb, ga, gb, pool_w_bf, pool_scale, w_out_bf, norm_ffn, w_router, b_router, tm, prompt,
           tiles_per_seq):
    n = h.shape[0]
    row = lambda i: (i, 0)
    const2 = lambda i: (0, 0)
    if prompt:
        per = tm // POOL_HALO
        hist_spec = pl.BlockSpec((POOL_HALO, POOL_WIDTH), lambda i: (jnp.maximum(i * per - 1, 0), 0))
        scratch = [pltpu.VMEM((POOL_HALO + tm, POOL_WIDTH), F32)]
    else:
        hist_spec = pl.BlockSpec((POOL_STATE_LEN, tm, POOL_WIDTH), lambda i: (0, i, 0))
        scratch = []
    outs = [
        jax.ShapeDtypeStruct((n, D_MODEL), F32),
        jax.ShapeDtypeStruct((n * ROW_CHUNKS, LANES), F32),
        jax.ShapeDtypeStruct((n, TOP_K), jnp.int32),
        jax.ShapeDtypeStruct((n, TOP_K), F32),
    ]
    out_specs = [pl.BlockSpec((tm, o.shape[1]), row) for o in outs]
    out_specs[1] = pl.BlockSpec((tm * ROW_CHUNKS, LANES), row)
    return pl.pallas_call(
        functools.partial(_merge_kernel, prompt=prompt, tiles_per_seq=tiles_per_seq),
        out_shape=outs,
        grid=(n // tm,),
        in_specs=[
            pl.BlockSpec((tm, D_MODEL), row),
            pl.BlockSpec((tm, POOL_WIDTH), row),
            hist_spec,
            pl.BlockSpec((tm, D_MODEL), row),
            pl.BlockSpec((tm, D_MODEL), row),
            pl.BlockSpec((tm, D_MODEL), row),
            pl.BlockSpec(pool_w_bf.shape, lambda i: (0, 0, 0)),
            pl.BlockSpec((1, D_MODEL), const2),
            pl.BlockSpec((D_MODEL, D_MODEL), const2),
            pl.BlockSpec((1, D_MODEL), const2),
            pl.BlockSpec((D_MODEL, N_EXPERTS), const2),
            pl.BlockSpec((1, N_EXPERTS), const2),
        ],
        out_specs=out_specs,
        scratch_shapes=scratch,
        compiler_params=_cparams(("parallel",)),
        name="merge_prompt" if prompt else "merge_sample",
    )(h, u, hist, yb, ga, gb, pool_w_bf, pool_scale.reshape(1, D_MODEL), w_out_bf,
      norm_ffn.reshape(1, D_MODEL), w_router, b_router.reshape(1, N_EXPERTS))


def _row_tile(ref, row):
    return ref.at[pl.ds(pl.multiple_of(row * ROW_CHUNKS, ROW_CHUNKS), ROW_CHUNKS), :]


def _dispatch_kernel(zs_ref, dest_ref, xp_ref, xs_ref, o_hbm, zero_ref, zsem, sem, *, np_tiles, bm):
    i = pl.program_id(0)
    tn = xp_ref.shape[0] // ROW_CHUNKS

    def zero_copy(e):
        start = pl.multiple_of(zs_ref[e] * ROW_CHUNKS, ROW_CHUNKS)
        return pltpu.make_async_copy(zero_ref, o_hbm.at[pl.ds(start, bm * ROW_CHUNKS), :], zsem)

    @pl.when(i == 0)
    def _():
        zero_ref[...] = jnp.zeros(zero_ref.shape, F32)
        for e in range(zs_ref.shape[0]):
            @pl.when(zs_ref[e] >= 0)
            def _():
                zero_copy(e).start()
        for e in range(zs_ref.shape[0]):
            @pl.when(zs_ref[e] >= 0)
            def _():
                zero_copy(e).wait()

    def scatter(src_ref):
        def issue(t, carry):
            for kk in range(TOP_K):
                pltpu.make_async_copy(_row_tile(src_ref, t), _row_tile(o_hbm, dest_ref[0, t * TOP_K + kk]), sem).start()
            return carry

        lax.fori_loop(0, tn, issue, 0, unroll=4)
        for kk in range(TOP_K):
            pltpu.make_async_copy(src_ref, src_ref, sem).wait()

    @pl.when(i < np_tiles)
    def _():
        scatter(xp_ref)

    @pl.when(i >= np_tiles)
    def _():
        scatter(xs_ref)


def _dispatch(zero_starts, dest, xn_p, xn_s, rows, tn, bm):
    np_tiles = xn_p.shape[0] // (tn * ROW_CHUNKS)
    ns_tiles = xn_s.shape[0] // (tn * ROW_CHUNKS)
    steps = np_tiles + ns_tiles
    grid_spec = pltpu.PrefetchScalarGridSpec(
        num_scalar_prefetch=1,
        grid=(steps,),
        in_specs=[
            pl.BlockSpec((None, 1, tn * TOP_K), lambda i, zs: (i, 0, 0), memory_space=pltpu.SMEM),
            pl.BlockSpec((tn * ROW_CHUNKS, LANES), lambda i, zs: (jnp.minimum(i, np_tiles - 1), 0)),
            pl.BlockSpec((tn * ROW_CHUNKS, LANES), lambda i, zs: (jnp.maximum(i - np_tiles, 0), 0)),
        ],
        out_specs=pl.BlockSpec(memory_space=pl.ANY),
        scratch_shapes=[pltpu.VMEM((bm * ROW_CHUNKS, LANES), F32), pltpu.SemaphoreType.DMA, pltpu.SemaphoreType.DMA],
    )
    return pl.pallas_call(
        functools.partial(_dispatch_kernel, np_tiles=np_tiles, bm=bm),
        out_shape=jax.ShapeDtypeStruct((rows * ROW_CHUNKS, LANES), F32),
        grid_spec=grid_spec,
        compiler_params=_cparams(("arbitrary",)),
        name="moe_dispatch",
    )(zero_starts, dest.reshape(steps, 1, tn * TOP_K), xn_p, xn_s)


def _ffn_kernel(be_ref, nu_ref, x_ref, wg_ref, bg_ref, wu_ref, bu_ref, wd_ref, bd_ref, y_ref):
    i = pl.program_id(0)
    bm = x_ref.shape[0] // ROW_CHUNKS

    @pl.when(i < nu_ref[0])
    def _():
        xb = jnp.concatenate([_chunk_rows(x_ref, c, bm).astype(BF16) for c in range(ROW_CHUNKS)], axis=1)
        g = jnp.dot(xb, wg_ref[...], preferred_element_type=F32) + bg_ref[...]
        u = jnp.dot(xb, wu_ref[...], preferred_element_type=F32) + bu_ref[...]
        g = jnp.minimum(g, SWIGLU_LIMIT)
        u = jnp.clip(u, -SWIGLU_LIMIT, SWIGLU_LIMIT)
        act = g * jax.nn.sigmoid(SWIGLU_ALPHA * g) * (u + 1.0)
        y = jnp.dot(act.astype(BF16), wd_ref[...], preferred_element_type=F32) + bd_ref[...]
        for c in range(ROW_CHUNKS):
            y_ref[pl.ds(c, bm, stride=ROW_CHUNKS), :] = y[:, c * LANES:(c + 1) * LANES]

    @pl.when(i >= nu_ref[0])
    def _():
        y_ref[...] = jnp.zeros(y_ref.shape, F32)


def _ffn(blk_e, n_used, xs, wg, bg, wu, bu, wd, bd, bm):
    n_blocks = xs.shape[0] // (bm * ROW_CHUNKS)
    w_spec = pl.BlockSpec((None, D_MODEL, D_MODEL), lambda i, be, nu: (be[i], 0, 0))
    b_spec = pl.BlockSpec((None, 1, D_MODEL), lambda i, be, nu: (be[i], 0, 0))
    x_spec = pl.BlockSpec((bm * ROW_CHUNKS, LANES), lambda i, be, nu: (jnp.minimum(i, nu[0] - 1), 0))
    y_spec = pl.BlockSpec((bm * ROW_CHUNKS, LANES), lambda i, be, nu: (i, 0))
    grid_spec = pltpu.PrefetchScalarGridSpec(
        num_scalar_prefetch=2,
        grid=(n_blocks,),
        in_specs=[x_spec, w_spec, b_spec, w_spec, b_spec, w_spec, b_spec],
        out_specs=y_spec,
    )
    e = bg.shape[0]
    return pl.pallas_call(
        _ffn_kernel,
        out_shape=jax.ShapeDtypeStruct(xs.shape, F32),
        grid_spec=grid_spec,
        compiler_params=_cparams(("arbitrary",)),
        name="moe_ffn",
    )(blk_e, n_used, xs, wg, bg.reshape(e, 1, D_MODEL), wu, bu.reshape(e, 1, D_MODEL), wd, bd.reshape(e, 1, D_MODEL))


def _combine_kernel(dest_ref, dnext_ref, hp_ref, hs_ref, wp_ref, ws_ref, nf_ref, ys_hbm, yp_ref, ysm_ref, buf_ref, sem,
                    *, np_tiles):
    i = pl.program_id(0)
    tn = hp_ref.shape[0]
    slot = i % 2

    def gather(d_ref, sl):
        def issue(t, carry):
            for kk in range(TOP_K):
                pltpu.make_async_copy(_row_tile(ys_hbm, d_ref[0, t * TOP_K + kk]), _row_tile(buf_ref.at[sl, kk], t),
                                      sem.at[sl]).start()
            return carry

        lax.fori_loop(0, tn, issue, 0, unroll=4)

    @pl.when(i == 0)
    def _():
        gather(dest_ref, 0)

    @pl.when(i + 1 < pl.num_programs(0))
    def _():
        gather(dnext_ref, 1 - slot)

    for kk in range(TOP_K):
        pltpu.make_async_copy(buf_ref.at[slot, kk], buf_ref.at[slot, kk], sem.at[slot]).wait()

    def finish(h_ref, w_ref, out_ref):
        w = w_ref[...]
        h = h_ref[...]
        cols = []
        for c in range(ROW_CHUNKS):
            acc = h[:, c * LANES:(c + 1) * LANES]
            for kk in range(TOP_K):
                acc = acc + w[:, kk:kk + 1] * _chunk_rows(buf_ref.at[slot, kk], c, tn)
            cols.append(acc)
        out_ref[...] = _rms(jnp.concatenate(cols, axis=1)) * nf_ref[...]

    @pl.when(i < np_tiles)
    def _():
        finish(hp_ref, wp_ref, yp_ref)

    @pl.when(i >= np_tiles)
    def _():
        finish(hs_ref, ws_ref, ysm_ref)


def _combine(dest, h_p, h_s, w_p, w_s, norm_final, ys, tn):
    np_tiles = h_p.shape[0] // tn
    ns_tiles = h_s.shape[0] // tn
    steps = np_tiles + ns_tiles
    p_map = lambda i: (jnp.minimum(i, np_tiles - 1), 0)
    s_map = lambda i: (jnp.maximum(i - np_tiles, 0), 0)
    dest3 = dest.reshape(steps, 1, tn * TOP_K)
    return pl.pallas_call(
        functools.partial(_combine_kernel, np_tiles=np_tiles),
        out_shape=[jax.ShapeDtypeStruct(h_p.shape, F32), jax.ShapeDtypeStruct(h_s.shape, F32)],
        grid=(steps,),
        in_specs=[
            pl.BlockSpec((None, 1, tn * TOP_K), lambda i: (i, 0, 0), memory_space=pltpu.SMEM),
            pl.BlockSpec((None, 1, tn * TOP_K), lambda i: (jnp.minimum(i + 1, steps - 1), 0, 0),
                         memory_space=pltpu.SMEM),
            pl.BlockSpec((tn, D_MODEL), p_map),
            pl.BlockSpec((tn, D_MODEL), s_map),
            pl.BlockSpec((tn, TOP_K), p_map),
            pl.BlockSpec((tn, TOP_K), s_map),
            pl.BlockSpec((1, D_MODEL), lambda i: (0, 0)),
            pl.BlockSpec(memory_space=pl.ANY),
        ],
        out_specs=[pl.BlockSpec((tn, D_MODEL), p_map), pl.BlockSpec((tn, D_MODEL), s_map)],
        scratch_shapes=[pltpu.VMEM((2, TOP_K, tn * ROW_CHUNKS, LANES), F32), pltpu.SemaphoreType.DMA((2,))],
        compiler_params=_cparams(("arbitrary",)),
        name="moe_combine",
    )(dest3, dest3, h_p, h_s, w_p, w_s, norm_final.reshape(1, D_MODEL), ys)


def _moe_final(h_p, xn_p, ti_p, tw_p, h_s, xn_s, ti_s, tw_s, wg, bg, wu, bu, wd, bd, norm_final):
    n_p, n_s = h_p.shape[0], h_s.shape[0]
    n = n_p + n_s
    nk = n * TOP_K
    bm = MOE_BLOCK
    tn = min(MOE_TOKENS, n_s)
    top_i = jnp.concatenate([ti_p, ti_s], axis=0)
    sel = (top_i[:, :, None] == jnp.arange(N_EXPERTS, dtype=jnp.int32)[None, None, :]).astype(jnp.int32).sum(1)
    incl = jnp.cumsum(sel, axis=0)
    counts = incl[-1]
    rank = jnp.take_along_axis(incl - sel, top_i, axis=1)
    padded = (counts + bm - 1) // bm * bm
    pad_end = jnp.cumsum(padded)
    start_pad = pad_end - padded
    dest = (start_pad[top_i] + rank).astype(jnp.int32)
    n_blocks = -(-nk // bm) + N_EXPERTS
    rows = n_blocks * bm
    blk_start = jnp.arange(n_blocks, dtype=jnp.int32) * bm
    blk_e = jnp.minimum(jnp.sum((pad_end[None, :] <= blk_start[:, None]).astype(jnp.int32), axis=1),
                        N_EXPERTS - 1).astype(jnp.int32)
    n_used = (pad_end[-1] // bm).astype(jnp.int32).reshape(1)
    tail_blk = n_blocks - 1 - jnp.arange(N_EXPERTS, dtype=jnp.int32)
    zero_starts = jnp.concatenate([jnp.where(padded > 0, pad_end - bm, -1),
                                   jnp.where(tail_blk >= n_used[0], tail_blk * bm, -1)]).astype(jnp.int32)

    xs = _dispatch(zero_starts, dest, xn_p, xn_s, rows, tn, bm)
    ys = _ffn(blk_e, n_used, xs, wg, bg, wu, bu, wd, bd, bm)
    return _combine(dest, h_p, h_s, tw_p, tw_s, norm_final, ys, tn)


def kernel(x_prompt, x_sample, cache_k, cache_v, state_pool, page_table, norm_mix, w_in, pool_w, pool_scale,
           lambda_q1, lambda_k1, lambda_q2, lambda_k2, subln_gain, w_out, norm_ffn, w_router, b_router,
           w_gate, b_gate, w_up, b_up, w_down, b_down, norm_final):
    bp, lp, d = x_prompt.shape
    db = x_sample.shape[0]
    layer = 0
    n_pool = cache_k.shape[1]

    w_in_bf = w_in[layer].astype(BF16)
    w_out_bf = w_out[layer].astype(BF16)
    pool_w_bf = pool_w[layer].astype(BF16)
    wg = w_gate[layer].astype(BF16)
    wu = w_up[layer].astype(BF16)
    wd = w_down[layer].astype(BF16)
    lamv = jnp.stack([lambda_q1[layer], lambda_k1[layer], lambda_q2[layer], lambda_k2[layer]]).astype(F32)
    gain = subln_gain[layer].reshape(1, V_DIM)

    cos_s, sin_s = _rope_tables(jnp.full((db,), PAST_LEN))
    xs = x_sample.reshape(db, d)
    us, qbs, kts, kbs, vs, vbs, gas, gbs = _proj(xs, norm_mix[layer], w_in_bf, cos_s, sin_s, db, None)
    cache_kt = jnp.transpose(cache_k[layer], (0, 2, 3, 4, 1)).reshape(n_pool, QK_WIDTH, PAGE_SIZE)
    cache_vr = cache_v[layer].reshape(n_pool, PAGE_SIZE * N_HEADS, V_DIM)
    ybs = _attn_sample(page_table, lamv, gain, qbs, kbs, vbs, cache_kt, cache_vr)
    state_t = jnp.transpose(state_pool[layer], (1, 0, 2))
    h2s, xns, tis, tws = _merge(xs, us, state_t, ybs, gas, gbs, pool_w_bf, pool_scale[layer], w_out_bf,
                                norm_ffn[layer], w_router[layer], b_router[layer], db, False, 1)
    k_sample = jnp.transpose(kts.reshape(QK_WIDTH, db), (1, 0)).reshape(1, db, 1, N_HEADS, 2, HEAD_DIM)
    v_sample = vs.reshape(1, db, 1, N_HEADS, V_DIM)
    pool_sample = jnp.concatenate([state_pool[layer][:, 1:, :], us[:, None, :]], axis=1)[None]

    tm = min(ROW_TILE, lp)
    tq = min(Q_TILE, lp)
    cos_p, sin_p = _rope_tables(jnp.arange(lp))
    xp = x_prompt.reshape(bp * lp, d)
    u, qb, kt, kbt, v, vb, ga, gb = _proj(xp, norm_mix[layer], w_in_bf, cos_p, sin_p, tm, lp)
    yb = _attn_prompt(lamv, gain, qb.reshape(bp, lp, QK_WIDTH), kbt, vb.reshape(bp, lp, V_WIDTH), tq)
    h2, xn, top_i, top_w = _merge(xp, u, u, yb.reshape(bp * lp, V_WIDTH), ga, gb, pool_w_bf, pool_scale[layer],
                                  w_out_bf, norm_ffn[layer], w_router[layer], b_router[layer], tm, True, lp // tm)
    pg = kt.shape[2]
    k_prompt = jnp.transpose(kt.reshape(bp, lp // pg, N_HEADS, 2, HEAD_DIM, pg), (0, 1, 5, 2, 3, 4))
    k_prompt = k_prompt.reshape(1, bp, lp // PAGE_SIZE, PAGE_SIZE, N_HEADS, 2, HEAD_DIM)
    v_prompt = v.reshape(1, bp, lp // PAGE_SIZE, PAGE_SIZE, N_HEADS, V_DIM)
    pool_prompt = u.reshape(bp, lp, POOL_WIDTH)[:, lp - POOL_STATE_LEN:, :][None]

    y_p, y_s = _moe_final(h2, xn, top_i, top_w, h2s, xns, tis, tws,
                          wg, b_gate[layer], wu, b_up[layer], wd, b_down[layer], norm_final)
    y_prompt = y_p.reshape(bp, lp, d)
    y_sample = y_s.reshape(db, 1, d)

    return (y_prompt, y_sample, k_prompt, v_prompt, pool_prompt, k_sample, v_sample, pool_sample)
```

```python
import functools
import math

import numpy as np
import jax
import jax.numpy as jnp
from jax import lax
from jax.experimental import pallas as pl
from jax.experimental.pallas import tpu as pltpu

F32 = jnp.float32
BF16 = jnp.bfloat16

N_HEADS = 8
HEAD_DIM = 64
V_DIM = 128
D_MODEL = 1024
POOL_WIDTH = 512
POOL_WINDOWS = (2, 4, 8, 16)
POOL_GROUP_DIM = 128
POOL_OUT_DIM = 256
POOL_STATE_LEN = 15
POOL_HALO = 16
QK_WIDTH = 1024
V_WIDTH = 1024
N_EXPERTS = 32
TOP_K = 4
SWIGLU_LIMIT = 7.0
SWIGLU_ALPHA = 1.702
ROPE_THETA = 10000.0
RMS_EPS = 1e-6
PAST_LEN = 8192
PAGE_SIZE = 128
LAM_INIT = 0.8 - 0.6 * math.exp(-0.3 * 0)

LANES = 128
SUBLANES = 8
ROW_CHUNKS = D_MODEL // LANES
VMEM_LIMIT = 56 * 2**20

ROW_TILE = 256
Q_TILE = 512
SOFTMAX_ROWS = 32
PAGES_PER_STEP = 8
MOE_BLOCK = 512
MOE_TOKENS = 128
M_FLOOR = -1e30


def _cparams(semantics):
    return pltpu.CompilerParams(dimension_semantics=semantics, vmem_limit_bytes=VMEM_LIMIT)


def _rms(x, eps=RMS_EPS):
    return x * lax.rsqrt(jnp.mean(x * x, axis=-1, keepdims=True) + eps)


def _chunk_rows(ref, c, n):
    return ref[pl.ds(c, n, stride=ROW_CHUNKS), :]


def _proj_kernel(x_ref, g_ref, w_ref, cos_ref, sin_ref,
                 u_ref, q_ref, kt_ref, kb_ref, v_ref, vb_ref, ga_ref, gb_ref, *, kb_transposed):
    tm = x_ref.shape[0]
    xn = (_rms(x_ref[...]) * g_ref[...]).astype(BF16)

    def sec(a, b):
        return jnp.dot(xn, w_ref[:, a:b], preferred_element_type=F32)

    a1 = POOL_WIDTH
    a2 = a1 + QK_WIDTH
    a3 = a2 + QK_WIDTH
    a4 = a3 + V_WIDTH
    a5 = a4 + D_MODEL
    a6 = a5 + D_MODEL

    u_ref[...] = sec(0, a1)

    cos = cos_ref[...]
    sin = sin_ref[...]
    lane = lax.broadcasted_iota(jnp.int32, (tm, LANES), 1)
    first_half = (lane % HEAD_DIM) < (HEAD_DIM // 2)

    def rope_chunk(zc):
        rot = jnp.where(first_half, pltpu.roll(zc, LANES - HEAD_DIM // 2, 1), pltpu.roll(zc, HEAD_DIM // 2, 1))
        return zc * cos + rot * sin

    zq = sec(a1, a2)
    for c in range(QK_WIDTH // LANES):
        sl = slice(c * LANES, (c + 1) * LANES)
        q_ref[:, sl] = (rope_chunk(zq[:, sl]) * (HEAD_DIM ** -0.5)).astype(BF16)
    zk = sec(a2, a3)
    pg = kt_ref.shape[2]
    for c in range(QK_WIDTH // LANES):
        sl = slice(c * LANES, (c + 1) * LANES)
        r = rope_chunk(zk[:, sl])
        if not kb_transposed:
            kb_ref[:, sl] = r.astype(BF16)
        for p in range(tm // pg):
            rt = r[p * pg:(p + 1) * pg, :].T
            kt_ref[p, sl, :] = rt
            if kb_transposed:
                kb_ref[c, :, p * pg:(p + 1) * pg] = rt.astype(BF16)
    zv = sec(a3, a4)
    vb_ref[...] = zv.astype(BF16)
    for h in range(N_HEADS):
        v_ref[pl.ds(h, tm, stride=N_HEADS), :] = zv[:, h * V_DIM:(h + 1) * V_DIM]
    ga_ref[...] = jax.nn.sigmoid(sec(a4, a5))
    gb_ref[...] = jax.nn.sigmoid(sec(a5, a6))


def _rope_tables(pos):
    inv = jnp.power(ROPE_THETA, -jnp.arange(0, HEAD_DIM, 2, dtype=F32) / HEAD_DIM)
    ang = pos.astype(F32)[:, None] * inv[None, :]
    cos = jnp.concatenate([jnp.cos(ang)] * 4, -1)
    sin = jnp.concatenate([-jnp.sin(ang), jnp.sin(ang)] * 2, -1)
    return cos, sin


def _proj(x2d, norm_g, w_in_bf, cos, sin, tm, seq_len):
    n = x2d.shape[0]
    n_pos_tiles = cos.shape[0] // tm
    row = lambda i: (i, 0)
    const = lambda i: (0, 0)
    in_width = w_in_bf.shape[1]
    pg = min(PAGE_SIZE, tm)
    kb_transposed = seq_len is not None
    if kb_transposed:
        tps = seq_len // tm
        kb_shape = jax.ShapeDtypeStruct((n // seq_len, N_HEADS, 2 * HEAD_DIM, seq_len), BF16)
        kb_spec = pl.BlockSpec((None, N_HEADS, 2 * HEAD_DIM, tm), lambda i: (i // tps, 0, 0, i % tps))
    else:
        kb_shape = jax.ShapeDtypeStruct((n, QK_WIDTH), BF16)
        kb_spec = pl.BlockSpec((tm, QK_WIDTH), row)
    outs = [
        jax.ShapeDtypeStruct((n, POOL_WIDTH), F32),
        jax.ShapeDtypeStruct((n, QK_WIDTH), BF16),
        jax.ShapeDtypeStruct((n // pg, QK_WIDTH, pg), F32),
        kb_shape,
        jax.ShapeDtypeStruct((n * N_HEADS, V_DIM), F32),
        jax.ShapeDtypeStruct((n, V_WIDTH), BF16),
        jax.ShapeDtypeStruct((n, D_MODEL), F32),
        jax.ShapeDtypeStruct((n, D_MODEL), F32),
    ]
    out_specs = [pl.BlockSpec((tm, o.shape[-1]), row) for o in outs]
    out_specs[2] = pl.BlockSpec((tm // pg, QK_WIDTH, pg), lambda i: (i, 0, 0))
    out_specs[3] = kb_spec
    out_specs[4] = pl.BlockSpec((tm * N_HEADS, V_DIM), row)
    return pl.pallas_call(
        functools.partial(_proj_kernel, kb_transposed=kb_transposed),
        out_shape=outs,
        grid=(n // tm,),
        in_specs=[
            pl.BlockSpec((tm, D_MODEL), row),
            pl.BlockSpec((1, D_MODEL), const),
            pl.BlockSpec((D_MODEL, in_width), const),
            pl.BlockSpec((tm, LANES), lambda i: (i % n_pos_tiles, 0)),
            pl.BlockSpec((tm, LANES), lambda i: (i % n_pos_tiles, 0)),
        ],
        out_specs=out_specs,
        compiler_params=_cparams(("parallel",)),
        name="proj",
    )(x2d, norm_g.reshape(1, D_MODEL), w_in_bf, cos, sin)


def _lam_value(lamv_ref):
    lv = lamv_ref[...]
    a = jnp.sum(lv[0:1] * lv[1:2], axis=-1, keepdims=True)
    b = jnp.sum(lv[2:3] * lv[3:4], axis=-1, keepdims=True)
    return jnp.exp(a) - jnp.exp(b) + LAM_INIT


def _subln(o, gain):
    return (_rms(o) * gain) * (1.0 - LAM_INIT)


def _attn_kernel(tab_ref, lamv_ref, gain_ref, q_ref, kt_ref, v_ref, o_ref,
                 qq_ref, m_ref, acc_ref, s_ref, p_ref, al_ref, *, tq, n_full, n_pairs):
    nq = q_ref.shape[0] // tq
    lane = lax.broadcasted_iota(jnp.int32, (tq, LANES), 1)
    for i in range(nq):
        q = q_ref[i * tq:(i + 1) * tq, :]
        zero = jnp.zeros_like(q)
        qq_ref[2 * i * tq:(2 * i + 1) * tq, :] = jnp.where(lane < HEAD_DIM, q, zero)
        qq_ref[(2 * i + 1) * tq:(2 * i + 2) * tq, :] = jnp.where(lane >= HEAD_DIM, q, zero)
    m_ref[...] = jnp.full(m_ref.shape, M_FLOOR, F32)
    acc_ref[...] = jnp.zeros(acc_ref.shape, F32)
    s_ref[...] = jnp.full(s_ref.shape, -jnp.inf, F32)
    p_ref[...] = jnp.zeros(p_ref.shape, BF16)
    al_ref[...] = jnp.ones(al_ref.shape, F32)

    def stage_pv(i, t):
        row0 = pl.multiple_of(i * 2 * tq, 2 * tq)
        col0 = pl.multiple_of(t * tq, tq)
        v_t = v_ref[pl.ds(col0, tq), :]
        v_e = jnp.concatenate([v_t, jnp.ones_like(v_t)], axis=1)
        pv = jnp.dot(p_ref[...], v_e, preferred_element_type=F32)
        al = al_ref[...]
        al2 = jnp.concatenate([al, al], axis=1)
        acc_ref[pl.ds(row0, 2 * tq), :] = al2 * acc_ref[pl.ds(row0, 2 * tq), :] + pv

    def stage_sm(i):
        row0 = pl.multiple_of(i * 2 * tq, 2 * tq)
        r = SOFTMAX_ROWS
        for c in range(2 * tq // r):
            rows = slice(c * r, (c + 1) * r)
            s = s_ref[rows, :]
            m_prev = m_ref[pl.ds(row0 + c * r, r), :]
            m_new = jnp.maximum(m_prev, jnp.max(s, axis=1, keepdims=True))
            al_ref[rows, :] = jnp.exp(m_prev - m_new)
            p_ref[rows, :] = jnp.exp(s - jnp.tile(m_new, (1, tq // LANES))).astype(BF16)
            m_ref[pl.ds(row0 + c * r, r), :] = m_new

    def stage_qk(i, t, it, masked):
        row0 = pl.multiple_of(i * 2 * tq, 2 * tq)
        col0 = pl.multiple_of(t * tq, tq)
        s = jnp.dot(qq_ref[pl.ds(row0, 2 * tq), :], kt_ref[:, pl.ds(col0, tq)], preferred_element_type=F32)
        if masked:
            rr = lax.broadcasted_iota(jnp.int32, (2 * tq, tq), 0)
            cc = lax.broadcasted_iota(jnp.int32, (2 * tq, tq), 1)
            rr = jnp.where(rr >= tq, rr - tq, rr) + jnp.where(it < n_pairs, 0, -(tq + 1))
            s = jnp.where(cc <= rr, s, -jnp.inf)
        s_ref[...] = s

    def pair(it):
        it = jnp.clip(it, 0, n_pairs - 1)
        return tab_ref[it], tab_ref[n_pairs + it]

    def trip(it, masked):
        i2, t2 = pair(it - 2)
        i1, _ = pair(it - 1)
        i0, t0 = pair(it)
        stage_pv(i2, t2)
        stage_sm(i1)
        stage_qk(i0, t0, it, masked)

    def full_body(it, carry):
        trip(it, False)
        return carry

    def diag_body(it, carry):
        trip(it, True)
        return carry

    if n_full:
        lax.fori_loop(0, n_full, full_body, 0)
    lax.fori_loop(n_full, n_pairs + 2, diag_body, 0)

    lam = _lam_value(lamv_ref)
    gain = gain_ref[...]
    for i in range(nq):
        a1 = acc_ref[2 * i * tq:(2 * i + 1) * tq, :]
        a2 = acc_ref[(2 * i + 1) * tq:(2 * i + 2) * tq, :]
        o = a1[:, :V_DIM] / a1[:, V_DIM:] - lam * (a2[:, :V_DIM] / a2[:, V_DIM:])
        o_ref[i * tq:(i + 1) * tq, :] = _subln(o, gain)


def _attn_prompt(lamv, gain, qb, kbt, vb, tq):
    b, L, _ = qb.shape
    nq = L // tq
    pairs = [(i, t) for i in range(nq) for t in range(i)] + [(i, i) for i in range(nq)]
    n_pairs = len(pairs)
    n_full = n_pairs - nq
    tab = jnp.asarray(np.array([p[0] for p in pairs] + [p[1] for p in pairs], np.int32))
    grid_spec = pltpu.PrefetchScalarGridSpec(
        num_scalar_prefetch=1,
        grid=(b, N_HEADS),
        in_specs=[
            pl.BlockSpec((4, HEAD_DIM), lambda bi, h, tab: (0, 0)),
            pl.BlockSpec((1, V_DIM), lambda bi, h, tab: (0, 0)),
            pl.BlockSpec((None, L, LANES), lambda bi, h, tab: (bi, 0, h)),
            pl.BlockSpec((None, None, LANES, L), lambda bi, h, tab: (bi, h, 0, 0)),
            pl.BlockSpec((None, L, LANES), lambda bi, h, tab: (bi, 0, h)),
        ],
        out_specs=pl.BlockSpec((None, L, LANES), lambda bi, h, tab: (bi, 0, h)),
        scratch_shapes=[
            pltpu.VMEM((2 * L, LANES), BF16),
            pltpu.VMEM((2 * L, LANES), F32),
            pltpu.VMEM((2 * L, 2 * V_DIM), F32),
            pltpu.VMEM((2 * tq, tq), F32),
            pltpu.VMEM((2 * tq, tq), BF16),
            pltpu.VMEM((2 * tq, LANES), F32),
        ],
    )
    return pl.pallas_call(
        functools.partial(_attn_kernel, tq=tq, n_full=n_full, n_pairs=n_pairs),
        out_shape=jax.ShapeDtypeStruct((b, L, V_WIDTH), F32),
        grid_spec=grid_spec,
        compiler_params=_cparams(("parallel", "parallel")),
        name="attn_prompt",
    )(tab, lamv, gain, qb, kbt, vb)


def _decode_kernel(pt_ref, lamv_ref, gain_ref, q_ref, kn_ref, vn_ref, ck_hbm, cv_hbm, o_ref,
                   kbuf_ref, vbuf_ref, sem_ref, qbd_ref, m_ref, l_ref, acc_ref):
    g = kbuf_ref.shape[1]
    n_maps = 2 * N_HEADS
    b = pl.program_id(0)
    j = pl.program_id(1)
    nj = pl.num_programs(1)
    step = b * nj + j
    slot = step % 2

    def fetch(bb, jj, sl):
        for t in range(g):
            page = pt_ref[bb, jj * g + t]
            pltpu.make_async_copy(ck_hbm.at[page], kbuf_ref.at[sl, t], sem_ref.at[0, sl]).start()
            pltpu.make_async_copy(cv_hbm.at[page], vbuf_ref.at[sl, t], sem_ref.at[1, sl]).start()

    @pl.when(step == 0)
    def _():
        fetch(0, 0, 0)

    wrap = j + 1 == nj

    @pl.when(step + 1 < pl.num_programs(0) * nj)
    def _():
        fetch(jnp.where(wrap, b + 1, b), jnp.where(wrap, 0, j + 1), 1 - slot)

    pltpu.make_async_copy(kbuf_ref.at[slot], kbuf_ref.at[slot], sem_ref.at[0, slot]).wait()
    pltpu.make_async_copy(vbuf_ref.at[slot], vbuf_ref.at[slot], sem_ref.at[1, slot]).wait()
    k_refs = [kbuf_ref.at[slot, t] for t in range(g)]
    v_refs = [vbuf_ref.at[slot, t] for t in range(g)]

    @pl.when(j == 0)
    def _():
        row = lax.broadcasted_iota(jnp.int32, (n_maps, QK_WIDTH), 0)
        col = lax.broadcasted_iota(jnp.int32, (n_maps, QK_WIDTH), 1)
        qrow = jnp.broadcast_to(q_ref[...].astype(F32), (n_maps, QK_WIDTH))
        qbd_ref[...] = jnp.where(col // HEAD_DIM == row, qrow, 0.0).astype(BF16)
        m_ref[...] = jnp.full(m_ref.shape, -jnp.inf, F32)
        l_ref[...] = jnp.zeros(l_ref.shape, F32)
        acc_ref[...] = jnp.zeros(acc_ref.shape, F32)

    qbd = qbd_ref[...]
    m = m_ref[...]
    s = jnp.concatenate([jnp.dot(qbd, k_refs[t][...].astype(BF16), preferred_element_type=F32) for t in range(g)],
                        axis=1)
    m_new = jnp.maximum(m, jnp.max(s, axis=1, keepdims=True))
    alpha = jnp.exp(m - m_new)
    p = jnp.exp(s - m_new).astype(BF16)
    l = alpha * l_ref[...] + jnp.sum(p.astype(F32), axis=1, keepdims=True)
    pv = []
    for h in range(N_HEADS):
        vh = jnp.concatenate([v_refs[t][pl.ds(h, PAGE_SIZE, stride=N_HEADS), :].astype(BF16) for t in range(g)],
                             axis=0)
        pv.append(jnp.dot(p, vh, preferred_element_type=F32))
    acc = alpha * acc_ref[...] + jnp.concatenate(pv, axis=1)
    m_ref[...] = m_new
    l_ref[...] = l
    acc_ref[...] = acc

    @pl.when(j == pl.num_programs(1) - 1)
    def _():
        kn = kn_ref[...].astype(F32)
        vn = vn_ref[...].astype(F32)
        s_self = jnp.sum(qbd.astype(F32) * kn, axis=1, keepdims=True)
        m_fin = jnp.maximum(m_new, s_self)
        a_fin = jnp.exp(m_new - m_fin)
        p_self = jnp.exp(s_self - m_fin)
        lf = a_fin * l + p_self
        accf = a_fin * acc + p_self.astype(BF16).astype(F32) * vn
        o = accf / lf
        lam = _lam_value(lamv_ref)
        gain = gain_ref[...]
        for h in range(N_HEADS):
            sl = slice(h * V_DIM, (h + 1) * V_DIM)
            oh = o[2 * h:2 * h + 1, sl] - lam * o[2 * h + 1:2 * h + 2, sl]
            o_ref[:, sl] = _subln(oh, gain)


def _attn_sample(page_table, lamv, gain, qb, k_new, v_new, cache_k, cache_v):
    db, n_pages = page_table.shape
    g = min(PAGES_PER_STEP, n_pages)
    n_maps = 2 * N_HEADS
    row_spec = pl.BlockSpec((None, 1, QK_WIDTH), lambda b, j, pt: (b, 0, 0))
    grid_spec = pltpu.PrefetchScalarGridSpec(
        num_scalar_prefetch=1,
        grid=(db, n_pages // g),
        in_specs=[
            pl.BlockSpec((4, HEAD_DIM), lambda b, j, pt: (0, 0)),
            pl.BlockSpec((1, V_DIM), lambda b, j, pt: (0, 0)),
            row_spec, row_spec, row_spec,
            pl.BlockSpec(memory_space=pl.ANY),
            pl.BlockSpec(memory_space=pl.ANY),
        ],
        out_specs=row_spec,
        scratch_shapes=[
            pltpu.VMEM((2, g, QK_WIDTH, PAGE_SIZE), F32),
            pltpu.VMEM((2, g, PAGE_SIZE * N_HEADS, V_DIM), F32),
            pltpu.SemaphoreType.DMA((2, 2)),
            pltpu.VMEM((n_maps, QK_WIDTH), BF16),
            pltpu.VMEM((n_maps, 1), F32),
            pltpu.VMEM((n_maps, 1), F32),
            pltpu.VMEM((n_maps, V_WIDTH), F32),
        ],
    )
    out = pl.pallas_call(
        _decode_kernel,
        out_shape=jax.ShapeDtypeStruct((db, 1, V_WIDTH), F32),
        grid_spec=grid_spec,
        compiler_params=_cparams(("arbitrary", "arbitrary")),
        name="attn_sample",
    )(page_table, lamv, gain, qb.reshape(db, 1, QK_WIDTH), k_new.reshape(db, 1, QK_WIDTH),
      v_new.reshape(db, 1, V_WIDTH), cache_k, cache_v)
    return out.reshape(db, V_WIDTH)


def _merge_kernel(*refs, prompt, tiles_per_seq):
    if prompt:
        (h_ref, u_ref, halo_ref, yb_ref, ga_ref, gb_ref, pw_ref, ps_ref, wo_ref, nf_ref, wr_ref, br_ref,
         h2_ref, xn_ref, ti_ref, tw_ref, ext_ref) = refs
    else:
        (h_ref, u_ref, st_ref, yb_ref, ga_ref, gb_ref, pw_ref, ps_ref, wo_ref, nf_ref, wr_ref, br_ref,
         h2_ref, xn_ref, ti_ref, tw_ref) = refs
    tm = h_ref.shape[0]
    u = u_ref[...]

    if prompt:
        seq_tile = pl.program_id(0) % tiles_per_seq
        halo = jnp.where(seq_tile == 0, 0.0, halo_ref[...])
        ext_ref[0:POOL_HALO, :] = halo
        ext_ref[POOL_HALO:, :] = u
        pos = seq_tile * tm + lax.broadcasted_iota(jnp.int32, (tm, 1), 0)

    ya_parts = []
    for gi, w in enumerate(POOL_WINDOWS):
        sl = slice(gi * POOL_GROUP_DIM, (gi + 1) * POOL_GROUP_DIM)
        cur = u[:, sl]
        wsum = cur
        if prompt:
            for d in range(1, w):
                wsum = wsum + ext_ref[POOL_HALO - d:POOL_HALO - d + tm, sl]
            cnt = jnp.minimum(pos + 1, w).astype(F32)
        else:
            for d in range(1, w):
                wsum = wsum + st_ref[POOL_STATE_LEN - d][:, sl]
            cnt = float(w)
        pooled = wsum / cnt - cur
        ya_parts.append(jnp.dot(pooled.astype(BF16), pw_ref[gi], preferred_element_type=F32))
    ya = jnp.concatenate(ya_parts, axis=1) * ps_ref[...]

    mix = ga_ref[...] * ya + gb_ref[...] * yb_ref[...]
    h2 = h_ref[...] + jnp.dot(mix.astype(BF16), wo_ref[...], preferred_element_type=F32)
    h2_ref[...] = h2
    xn = _rms(h2) * nf_ref[...]
    for c in range(ROW_CHUNKS):
        xn_ref[pl.ds(c, tm, stride=ROW_CHUNKS), :] = xn[:, c * LANES:(c + 1) * LANES]

    wr = wr_ref[...]
    wr_hi = wr.astype(BF16)
    wr_lo = (wr - wr_hi.astype(F32)).astype(BF16)
    xn_hi = xn.astype(BF16)
    xn_lo = (xn - xn_hi.astype(F32)).astype(BF16)
    logits = (jnp.dot(xn_hi, wr_hi, preferred_element_type=F32)
              + jnp.dot(xn_lo, wr_hi, preferred_element_type=F32)
              + jnp.dot(xn_hi, wr_lo, preferred_element_type=F32)) + br_ref[...]

    lane = lax.broadcasted_iota(jnp.int32, (tm, N_EXPERTS), 1).astype(F32)
    slot = lax.broadcasted_iota(jnp.int32, (tm, TOP_K), 1)
    vals = logits
    top_i = jnp.zeros((tm, TOP_K), F32)
    top_v = jnp.zeros((tm, TOP_K), F32)
    for kk in range(TOP_K):
        mx = jnp.max(vals, axis=1, keepdims=True)
        am = jnp.min(jnp.where(vals == mx, lane, float(N_EXPERTS)), axis=1, keepdims=True)
        top_i = jnp.where(slot == kk, am, top_i)
        top_v = jnp.where(slot == kk, mx, top_v)
        vals = jnp.where(lane == am, -jnp.inf, vals)
    e = jnp.exp(top_v - jnp.max(top_v, axis=1, keepdims=True))
    ti_ref[...] = top_i.astype(jnp.int32)
    tw_ref[...] = e / jnp.sum(e, axis=1, keepdims=True)


def _merge(h, u, hist, yb, ga, gb, pool_w_bf, pool_scale, w_out_bf, norm_ffn, w_router, b_router, tm, prompt,
           tiles_per_seq):
    n = h.shape[0]
    row = lambda i: (i, 0)
    const2 = lambda i: (0, 0)
    if prompt:
        per = tm // POOL_HALO
        hist_spec = pl.BlockSpec((POOL_HALO, POOL_WIDTH), lambda i: (jnp.maximum(i * per - 1, 0), 0))
        scratch = [pltpu.VMEM((POOL_HALO + tm, POOL_WIDTH), F32)]
    else:
        hist_spec = pl.BlockSpec((POOL_STATE_LEN, tm, POOL_WIDTH), lambda i: (0, i, 0))
        scratch = []
    outs = [
        jax.ShapeDtypeStruct((n, D_MODEL), F32),
        jax.ShapeDtypeStruct((n * ROW_CHUNKS, LANES), F32),
        jax.ShapeDtypeStruct((n, TOP_K), jnp.int32),
        jax.ShapeDtypeStruct((n, TOP_K), F32),
    ]
    out_specs = [pl.BlockSpec((tm, o.shape[1]), row) for o in outs]
    out_specs[1] = pl.BlockSpec((tm * ROW_CHUNKS, LANES), row)
    return pl.pallas_call(
        functools.partial(_merge_kernel, prompt=prompt, tiles_per_seq=tiles_per_seq),
        out_shape=outs,
        grid=(n // tm,),
        in_specs=[
            pl.BlockSpec((tm, D_MODEL), row),
            pl.BlockSpec((tm, POOL_WIDTH), row),
            hist_spec,
            pl.BlockSpec((tm, D_MODEL), row),
            pl.BlockSpec((tm, D_MODEL), row),
            pl.BlockSpec((tm, D_MODEL), row),
            pl.BlockSpec(pool_w_bf.shape, lambda i: (0, 0, 0)),
            pl.BlockSpec((1, D_MODEL), const2),
            pl.BlockSpec((D_MODEL, D_MODEL), const2),
            pl.BlockSpec((1, D_MODEL), const2),
            pl.BlockSpec((D_MODEL, N_EXPERTS), const2),
            pl.BlockSpec((1, N_EXPERTS), const2),
        ],
        out_specs=out_specs,
        scratch_shapes=scratch,
        compiler_params=_cparams(("parallel",)),
        name="merge_prompt" if prompt else "merge_sample",
    )(h, u, hist, yb, ga, gb, pool_w_bf, pool_scale.reshape(1, D_MODEL), w_out_bf,
      norm_ffn.reshape(1, D_MODEL), w_router, b_router.reshape(1, N_EXPERTS))


def _row_tile(ref, row):
    return ref.at[pl.ds(pl.multiple_of(row * ROW_CHUNKS, ROW_CHUNKS), ROW_CHUNKS), :]


def _dispatch_kernel(zs_ref, dest_ref, xp_ref, xs_ref, o_hbm, zero_ref, zsem, sem, *, np_tiles, bm):
    i = pl.program_id(0)
    tn = xp_ref.shape[0] // ROW_CHUNKS

    def zero_copy(e):
        start = pl.multiple_of(zs_ref[e] * ROW_CHUNKS, ROW_CHUNKS)
        return pltpu.make_async_copy(zero_ref, o_hbm.at[pl.ds(start, bm * ROW_CHUNKS), :], zsem)

    @pl.when(i == 0)
    def _():
        zero_ref[...] = jnp.zeros(zero_ref.shape, F32)
        for e in range(zs_ref.shape[0]):
            @pl.when(zs_ref[e] >= 0)
            def _():
                zero_copy(e).start()
        for e in range(zs_ref.shape[0]):
            @pl.when(zs_ref[e] >= 0)
            def _():
                zero_copy(e).wait()

    def scatter(src_ref):
        def issue(t, carry):
            for kk in range(TOP_K):
                pltpu.make_async_copy(_row_tile(src_ref, t), _row_tile(o_hbm, dest_ref[0, t * TOP_K + kk]),
                                      sem).start(priority=kk % 2)
            return carry

        lax.fori_loop(0, tn, issue, 0, unroll=4)
        for kk in range(TOP_K):
            pltpu.make_async_copy(src_ref, src_ref, sem).wait()

    @pl.when(i < np_tiles)
    def _():
        scatter(xp_ref)

    @pl.when(i >= np_tiles)
    def _():
        scatter(xs_ref)


def _dispatch(zero_starts, dest, xn_p, xn_s, rows, tn, bm):
    np_tiles = xn_p.shape[0] // (tn * ROW_CHUNKS)
    ns_tiles = xn_s.shape[0] // (tn * ROW_CHUNKS)
    steps = np_tiles + ns_tiles
    grid_spec = pltpu.PrefetchScalarGridSpec(
        num_scalar_prefetch=1,
        grid=(steps,),
        in_specs=[
            pl.BlockSpec((None, 1, tn * TOP_K), lambda i, zs: (i, 0, 0), memory_space=pltpu.SMEM),
            pl.BlockSpec((tn * ROW_CHUNKS, LANES), lambda i, zs: (jnp.minimum(i, np_tiles - 1), 0)),
            pl.BlockSpec((tn * ROW_CHUNKS, LANES), lambda i, zs: (jnp.maximum(i - np_tiles, 0), 0)),
        ],
        out_specs=pl.BlockSpec(memory_space=pl.ANY),
        scratch_shapes=[pltpu.VMEM((bm * ROW_CHUNKS, LANES), F32), pltpu.SemaphoreType.DMA, pltpu.SemaphoreType.DMA],
    )
    return pl.pallas_call(
        functools.partial(_dispatch_kernel, np_tiles=np_tiles, bm=bm),
        out_shape=jax.ShapeDtypeStruct((rows * ROW_CHUNKS, LANES), F32),
        grid_spec=grid_spec,
        compiler_params=_cparams(("arbitrary",)),
        name="moe_dispatch",
    )(zero_starts, dest.reshape(steps, 1, tn * TOP_K), xn_p, xn_s)


def _ffn_kernel(be_ref, nu_ref, x_ref, wg_ref, bg_ref, wu_ref, bu_ref, wd_ref, bd_ref, y_ref):
    i = pl.program_id(0)
    bm = x_ref.shape[0] // ROW_CHUNKS

    @pl.when(i < nu_ref[0])
    def _():
        xb = jnp.concatenate([_chunk_rows(x_ref, c, bm).astype(BF16) for c in range(ROW_CHUNKS)], axis=1)
        g = jnp.dot(xb, wg_ref[...], preferred_element_type=F32) + bg_ref[...]
        u = jnp.dot(xb, wu_ref[...], preferred_element_type=F32) + bu_ref[...]
        g = jnp.minimum(g, SWIGLU_LIMIT)
        u = jnp.clip(u, -SWIGLU_LIMIT, SWIGLU_LIMIT)
        act = g * jax.nn.sigmoid(SWIGLU_ALPHA * g) * (u + 1.0)
        y = jnp.dot(act.astype(BF16), wd_ref[...], preferred_element_type=F32) + bd_ref[...]
        for c in range(ROW_CHUNKS):
            y_ref[pl.ds(c, bm, stride=ROW_CHUNKS), :] = y[:, c * LANES:(c + 1) * LANES]

    @pl.when(i >= nu_ref[0])
    def _():
        y_ref[...] = jnp.zeros(y_ref.shape, F32)


def _ffn(blk_e, n_used, xs, wg, bg, wu, bu, wd, bd, bm):
    n_blocks = xs.shape[0] // (bm * ROW_CHUNKS)
    w_spec = pl.BlockSpec((None, D_MODEL, D_MODEL), lambda i, be, nu: (be[i], 0, 0))
    b_spec = pl.BlockSpec((None, 1, D_MODEL), lambda i, be, nu: (be[i], 0, 0))
    x_spec = pl.BlockSpec((bm * ROW_CHUNKS, LANES), lambda i, be, nu: (jnp.minimum(i, nu[0] - 1), 0))
    y_spec = pl.BlockSpec((bm * ROW_CHUNKS, LANES), lambda i, be, nu: (i, 0))
    grid_spec = pltpu.PrefetchScalarGridSpec(
        num_scalar_prefetch=2,
        grid=(n_blocks,),
        in_specs=[x_spec, w_spec, b_spec, w_spec, b_spec, w_spec, b_spec],
        out_specs=y_spec,
    )
    e = bg.shape[0]
    return pl.pallas_call(
        _ffn_kernel,
        out_shape=jax.ShapeDtypeStruct(xs.shape, F32),
        grid_spec=grid_spec,
        compiler_params=_cparams(("arbitrary",)),
        name="moe_ffn",
    )(blk_e, n_used, xs, wg, bg.reshape(e, 1, D_MODEL), wu, bu.reshape(e, 1, D_MODEL), wd, bd.reshape(e, 1, D_MODEL))


def _combine_kernel(dest_ref, dnext_ref, hp_ref, hs_ref, wp_ref, ws_ref, nf_ref, ys_hbm, yp_ref, ysm_ref, buf_ref, sem,
                    *, np_tiles):
    i = pl.program_id(0)
    tn = hp_ref.shape[0]
    slot = i % 2

    def gather(d_ref, sl):
        def issue(t, carry):
            for kk in range(TOP_K):
                pltpu.make_async_copy(_row_tile(ys_hbm, d_ref[0, t * TOP_K + kk]), _row_tile(buf_ref.at[sl, kk], t),
                                      sem.at[sl]).start(priority=kk % 2)
            return carry

        lax.fori_loop(0, tn, issue, 0, unroll=4)

    @pl.when(i == 0)
    def _():
        gather(dest_ref, 0)

    @pl.when(i + 1 < pl.num_programs(0))
    def _():
        gather(dnext_ref, 1 - slot)

    for kk in range(TOP_K):
        pltpu.make_async_copy(buf_ref.at[slot, kk], buf_ref.at[slot, kk], sem.at[slot]).wait()

    def finish(h_ref, w_ref, out_ref):
        w = w_ref[...]
        h = h_ref[...]
        cols = []
        for c in range(ROW_CHUNKS):
            acc = h[:, c * LANES:(c + 1) * LANES]
            for kk in range(TOP_K):
                acc = acc + w[:, kk:kk + 1] * _chunk_rows(buf_ref.at[slot, kk], c, tn)
            cols.append(acc)
        out_ref[...] = _rms(jnp.concatenate(cols, axis=1)) * nf_ref[...]

    @pl.when(i < np_tiles)
    def _():
        finish(hp_ref, wp_ref, yp_ref)

    @pl.when(i >= np_tiles)
    def _():
        finish(hs_ref, ws_ref, ysm_ref)


def _combine(dest, h_p, h_s, w_p, w_s, norm_final, ys, tn):
    np_tiles = h_p.shape[0] // tn
    ns_tiles = h_s.shape[0] // tn
    steps = np_tiles + ns_tiles
    p_map = lambda i: (jnp.minimum(i, np_tiles - 1), 0)
    s_map = lambda i: (jnp.maximum(i - np_tiles, 0), 0)
    dest3 = dest.reshape(steps, 1, tn * TOP_K)
    return pl.pallas_call(
        functools.partial(_combine_kernel, np_tiles=np_tiles),
        out_shape=[jax.ShapeDtypeStruct(h_p.shape, F32), jax.ShapeDtypeStruct(h_s.shape, F32)],
        grid=(steps,),
        in_specs=[
            pl.BlockSpec((None, 1, tn * TOP_K), lambda i: (i, 0, 0), memory_space=pltpu.SMEM),
            pl.BlockSpec((None, 1, tn * TOP_K), lambda i: (jnp.minimum(i + 1, steps - 1), 0, 0),
                         memory_space=pltpu.SMEM),
            pl.BlockSpec((tn, D_MODEL), p_map),
            pl.BlockSpec((tn, D_MODEL), s_map),
            pl.BlockSpec((tn, TOP_K), p_map),
            pl.BlockSpec((tn, TOP_K), s_map),
            pl.BlockSpec((1, D_MODEL), lambda i: (0, 0)),
            pl.BlockSpec(memory_space=pl.ANY),
        ],
        out_specs=[pl.BlockSpec((tn, D_MODEL), p_map), pl.BlockSpec((tn, D_MODEL), s_map)],
        scratch_shapes=[pltpu.VMEM((2, TOP_K, tn * ROW_CHUNKS, LANES), F32), pltpu.SemaphoreType.DMA((2,))],
        compiler_params=_cparams(("arbitrary",)),
        name="moe_combine",
    )(dest3, dest3, h_p, h_s, w_p, w_s, norm_final.reshape(1, D_MODEL), ys)


def _moe_final(h_p, xn_p, ti_p, tw_p, h_s, xn_s, ti_s, tw_s, wg, bg, wu, bu, wd, bd, norm_final):
    n_p, n_s = h_p.shape[0], h_s.shape[0]
    n = n_p + n_s
    nk = n * TOP_K
    bm = MOE_BLOCK
    tn = min(MOE_TOKENS, n_s)
    top_i = jnp.concatenate([ti_p, ti_s], axis=0)
    sel = (top_i[:, :, None] == jnp.arange(N_EXPERTS, dtype=jnp.int32)[None, None, :]).astype(jnp.int32).sum(1)
    incl = jnp.cumsum(sel, axis=0)
    counts = incl[-1]
    rank = jnp.take_along_axis(incl - sel, top_i, axis=1)
    padded = (counts + bm - 1) // bm * bm
    pad_end = jnp.cumsum(padded)
    start_pad = pad_end - padded
    dest = (start_pad[top_i] + rank).astype(jnp.int32)
    n_blocks = -(-nk // bm) + N_EXPERTS
    rows = n_blocks * bm
    blk_start = jnp.arange(n_blocks, dtype=jnp.int32) * bm
    blk_e = jnp.minimum(jnp.sum((pad_end[None, :] <= blk_start[:, None]).astype(jnp.int32), axis=1),
                        N_EXPERTS - 1).astype(jnp.int32)
    n_used = (pad_end[-1] // bm).astype(jnp.int32).reshape(1)
    tail_blk = n_blocks - 1 - jnp.arange(N_EXPERTS, dtype=jnp.int32)
    zero_starts = jnp.concatenate([jnp.where(padded > 0, pad_end - bm, -1),
                                   jnp.where(tail_blk >= n_used[0], tail_blk * bm, -1)]).astype(jnp.int32)

    xs = _dispatch(zero_starts, dest, xn_p, xn_s, rows, tn, bm)
    ys = _ffn(blk_e, n_used, xs, wg, bg, wu, bu, wd, bd, bm)
    return _combine(dest, h_p, h_s, tw_p, tw_s, norm_final, ys, tn)


def kernel(x_prompt, x_sample, cache_k, cache_v, state_pool, page_table, norm_mix, w_in, pool_w, pool_scale,
           lambda_q1, lambda_k1, lambda_q2, lambda_k2, subln_gain, w_out, norm_ffn, w_router, b_router,
           w_gate, b_gate, w_up, b_up, w_down, b_down, norm_final):
    bp, lp, d = x_prompt.shape
    db = x_sample.shape[0]
    layer = 0
    n_pool = cache_k.shape[1]

    w_in_bf = w_in[layer].astype(BF16)
    w_out_bf = w_out[layer].astype(BF16)
    pool_w_bf = pool_w[layer].astype(BF16)
    wg = w_gate[layer].astype(BF16)
    wu = w_up[layer].astype(BF16)
    wd = w_down[layer].astype(BF16)
    lamv = jnp.stack([lambda_q1[layer], lambda_k1[layer], lambda_q2[layer], lambda_k2[layer]]).astype(F32)
    gain = subln_gain[layer].reshape(1, V_DIM)

    cos_s, sin_s = _rope_tables(jnp.full((db,), PAST_LEN))
    xs = x_sample.reshape(db, d)
    us, qbs, kts, kbs, vs, vbs, gas, gbs = _proj(xs, norm_mix[layer], w_in_bf, cos_s, sin_s, db, None)
    cache_kt = jnp.transpose(cache_k[layer], (0, 2, 3, 4, 1)).reshape(n_pool, QK_WIDTH, PAGE_SIZE)
    cache_vr = cache_v[layer].reshape(n_pool, PAGE_SIZE * N_HEADS, V_DIM)
    ybs = _attn_sample(page_table, lamv, gain, qbs, kbs, vbs, cache_kt, cache_vr)
    state_t = jnp.transpose(state_pool[layer], (1, 0, 2))
    h2s, xns, tis, tws = _merge(xs, us, state_t, ybs, gas, gbs, pool_w_bf, pool_scale[layer], w_out_bf,
                                norm_ffn[layer], w_router[layer], b_router[layer], db, False, 1)
    k_sample = jnp.transpose(kts.reshape(QK_WIDTH, db), (1, 0)).reshape(1, db, 1, N_HEADS, 2, HEAD_DIM)
    v_sample = vs.reshape(1, db, 1, N_HEADS, V_DIM)
    pool_sample = jnp.concatenate([state_pool[layer][:, 1:, :], us[:, None, :]], axis=1)[None]

    tm = min(ROW_TILE, lp)
    tq = min(Q_TILE, lp)
    cos_p, sin_p = _rope_tables(jnp.arange(lp))
    xp = x_prompt.reshape(bp * lp, d)
    u, qb, kt, kbt, v, vb, ga, gb = _proj(xp, norm_mix[layer], w_in_bf, cos_p, sin_p, tm, lp)
    yb = _attn_prompt(lamv, gain, qb.reshape(bp, lp, QK_WIDTH), kbt, vb.reshape(bp, lp, V_WIDTH), tq)
    h2, xn, top_i, top_w = _merge(xp, u, u, yb.reshape(bp * lp, V_WIDTH), ga, gb, pool_w_bf, pool_scale[layer],
                                  w_out_bf, norm_ffn[layer], w_router[layer], b_router[layer], tm, True, lp // tm)
    pg = kt.shape[2]
    k_prompt = jnp.transpose(kt.reshape(bp, lp // pg, N_HEADS, 2, HEAD_DIM, pg), (0, 1, 5, 2, 3, 4))
    k_prompt = k_prompt.reshape(1, bp, lp // PAGE_SIZE, PAGE_SIZE, N_HEADS, 2, HEAD_DIM)
    v_prompt = v.reshape(1, bp, lp // PAGE_SIZE, PAGE_SIZE, N_HEADS, V_DIM)
    pool_prompt = u.reshape(bp, lp, POOL_WIDTH)[:, lp - POOL_STATE_LEN:, :][None]

    y_p, y_s = _moe_final(h2, xn, top_i, top_w, h2s, xns, tis, tws,
                          wg, b_gate[layer], wu, b_up[layer], wd, b_down[layer], norm_final)
    y_prompt = y_p.reshape(bp, lp, d)
    y_sample = y_s.reshape(db, 1, d)

    return (y_prompt, y_sample, k_prompt, v_prompt, pool_prompt, k_sample, v_sample, pool_sample)
```
